```python
import jax, jax.numpy as jnp
from jax import lax
import numpy as np

D_MODEL = 1024
BATCH = 2
SEQ = 8192
DEPTH = 2
DEC_BATCH = 128
DEC_SEQ = 4
PAST_LEN = 2048
PAGE_SIZE = 128

A_HEADS = 4
A_DK = 128
A_DV = 128
A_WIDTH_K = A_HEADS * A_DK
A_WIDTH_V = A_HEADS * A_DV
HGRN_CHUNK = 64
SB_HEADS = 8
SB_HEAD_DIM = 64
SB_WIDTH = SB_HEADS * SB_HEAD_DIM
SB_BLOCK = 128
SB_BIAS_HI = -2.0
SB_BIAS_LO = -8.0
C_CH = 512
CONV_W = 31
D_FF = 4 * D_MODEL
N_BRANCH = 3
EPS = 1e-6
IN_SIZES = (A_WIDTH_K, A_WIDTH_K, A_WIDTH_V, A_WIDTH_V, SB_WIDTH, SB_WIDTH, SB_WIDTH, 2 * C_CH, N_BRANCH * D_MODEL)
IN_WIDTH = sum(IN_SIZES)

kernel_name = 'hgrn2_stickbreak_conformer_gated_hybrid_step'


def _split_points():
    return [int(s) for s in np.cumsum(IN_SIZES)[:-1]]


def _rmsnorm(x, g):
    xf = x.astype(jnp.float32)
    y = xf * lax.rsqrt(jnp.mean(xf * xf, axis=-1, keepdims=True) + EPS) * g.astype(jnp.float32)
    return y.astype(x.dtype)


def _layernorm(x, g, b):
    xf = x.astype(jnp.float32)
    mu = jnp.mean(xf, axis=-1, keepdims=True)
    var = jnp.mean(jnp.square(xf - mu), axis=-1, keepdims=True)
    y = (xf - mu) * lax.rsqrt(var + EPS) * g.astype(jnp.float32) + b.astype(jnp.float32)
    return y.astype(x.dtype)


def _hgrn2_chunked(q, k, v, log_f, s0):
    Bn, T, H, DK = q.shape
    C = min(HGRN_CHUNK, T)
    n = -(-T // C)
    pad = n * C - T
    q, k, v, log_f = (a.astype(jnp.float32) for a in (q, k, v, log_f))
    if pad:
        pw = ((0, 0), (0, pad), (0, 0), (0, 0))
        q, k, v, log_f = (jnp.pad(a, pw) for a in (q, k, v, log_f))
    def chunks(a):
        return a.reshape(Bn, n, C, H, a.shape[-1]).transpose(1, 0, 3, 2, 4)
    tri = jnp.tril(jnp.ones((C, C), dtype=bool))

    def step(S, inp):
        qc, kc, vc, gc = inp
        b = jnp.cumsum(gc, axis=2)
        diff = b[:, :, :, None, :] - b[:, :, None, :, :]
        decay = jnp.exp(jnp.where(tri[:, :, None], diff, -jnp.inf))
        att = jnp.einsum('bhtd,bhsd,bhtsd->bhts', qc, kc, decay)
        o = jnp.einsum('bhts,bhsv->bhtv', att, vc) + jnp.einsum('bhtd,bhdv->bhtv', qc * jnp.exp(b), S)
        b_last = b[:, :, -1:, :]
        S_new = jnp.exp(b_last[:, :, 0, :])[..., None] * S + jnp.einsum('bhsd,bhsv->bhdv', kc * jnp.exp(b_last - b), vc)
        return S_new, o

    S, o = lax.scan(step, s0.astype(jnp.float32), (chunks(q), chunks(k), chunks(v), chunks(log_f)))
    o = o.transpose(1, 0, 3, 2, 4).reshape(Bn, n * C, H, v.shape[-1])[:, :T]
    return o, S


def _sb_block(q, k, v, bias, q_pos, k_pos):
    z = jnp.einsum('bhqd,bhkd->bhqk', q, k) * (SB_HEAD_DIM ** -0.5) + bias[None, :, None, None]
    mask = k_pos[None, :] < q_pos[:, None]
    log_beta = jax.nn.log_sigmoid(z)
    log_rest = jnp.where(mask, jax.nn.log_sigmoid(-z), 0.0)
    tail = lax.cumsum(log_rest, axis=3, reverse=True) - log_rest
    A = jnp.where(mask, jnp.exp(log_beta + tail), 0.0)
    return jnp.einsum('bhqk,bhkd->bhqd', A, v)


def _sb_attention(q, k, v, bias, q_pos, k_pos):
    Bn, Tq, H, d = q.shape
    qh = q.astype(jnp.float32).transpose(0, 2, 1, 3)
    kh = k.astype(jnp.float32).transpose(0, 2, 1, 3)
    vh = v.astype(jnp.float32).transpose(0, 2, 1, 3)
    bf = bias.astype(jnp.float32)
    blk = SB_BLOCK if Tq % SB_BLOCK == 0 else Tq
    nb = Tq // blk
    qblk = qh.reshape(Bn, H, nb, blk, d).transpose(2, 0, 1, 3, 4)
    pblk = q_pos.reshape(nb, blk)
    out = lax.map(lambda a: _sb_block(a[0], kh, vh, bf, a[1], k_pos), (qblk, pblk))
    return out.transpose(1, 0, 3, 2, 4).reshape(Bn, Tq, H, d).astype(q.dtype)


def _causal_dwconv(u_full, w, b):
    y = lax.conv_general_dilated(u_full, w[:, None, :].astype(u_full.dtype), window_strides=(1,), padding='VALID',
                                 dimension_numbers=('NWC', 'WIO', 'NWC'), feature_group_count=u_full.shape[-1])
    return y + b.astype(u_full.dtype)


def _mixer(h, l, W, lb, past):
    Bn, T, _ = h.shape
    dt = h.dtype
    proj = h @ W['w_in'][l]
    qa, fa, ia, ga, qb, kb, vb, uc, gl = jnp.split(proj, _split_points(), axis=-1)

    qa = jax.nn.silu(qa).reshape(Bn, T, A_HEADS, A_DK)
    ff = fa.astype(jnp.float32)
    if l == 0:
        log_f = jax.nn.log_sigmoid(ff)
    else:
        log_f = jnp.log(lb[l] + (1.0 - lb[l]) * jax.nn.sigmoid(ff))
    ka = -jnp.expm1(log_f)
    s0 = jnp.zeros((Bn, A_HEADS, A_DK, A_DV), jnp.float32) if past is None else past['s']
    oa, s_new = _hgrn2_chunked(qa, ka.reshape(Bn, T, A_HEADS, A_DK), ia.reshape(Bn, T, A_HEADS, A_DV),
                               log_f.reshape(Bn, T, A_HEADS, A_DK), s0)
    oa = _rmsnorm(oa.astype(dt), W['a_norm'][l]) * jax.nn.silu(ga.reshape(Bn, T, A_HEADS, A_DV))
    ya = oa.reshape(Bn, T, A_WIDTH_V) @ W['w_pa'][l]

    qb = qb.reshape(Bn, T, SB_HEADS, SB_HEAD_DIM)
    kb = kb.reshape(Bn, T, SB_HEADS, SB_HEAD_DIM)
    vb = vb.reshape(Bn, T, SB_HEADS, SB_HEAD_DIM)
    if past is None:
        k_all, v_all, offset = kb, vb, 0
    else:
        k_all = jnp.concatenate([past['k'].astype(dt), kb], axis=1)
        v_all = jnp.concatenate([past['v'].astype(dt), vb], axis=1)
        offset = past['k'].shape[1]
    q_pos = offset + jnp.arange(T, dtype=jnp.int32)
    k_pos = jnp.arange(k_all.shape[1], dtype=jnp.int32)
    ob = _sb_attention(qb, k_all, v_all, W['sb_bias'][l], q_pos, k_pos)
    yb = ob.reshape(Bn, T, SB_WIDTH) @ W['w_pb'][l]

    ua, ug = jnp.split(uc, 2, axis=-1)
    u = ua * jax.nn.sigmoid(ug)
    prev = jnp.zeros((Bn, CONV_W - 1, C_CH), dt) if past is None else past['conv'].astype(dt)
    u_full = jnp.concatenate([prev, u], axis=1)
    c = _causal_dwconv(u_full, W['conv_w'][l], W['conv_b'][l])
    c = jax.nn.silu(_layernorm(c, W['ln_g'][l], W['ln_b'][l]))
    yc = c @ W['w_pc'][l]
    conv_new = u_full[:, -(CONV_W - 1):]

    gates = jax.nn.sigmoid(gl.reshape(Bn, T, N_BRANCH, D_MODEL))
    m = gates[:, :, 0] * ya + gates[:, :, 1] * yb + gates[:, :, 2] * yc
    out = m @ W['w_o'][l]
    return out, (kb, vb, s_new.astype(dt), conv_new)


def _trunk(x, W, lb, cache):
    ks, vs, ss, cs = [], [], [], []
    for l in range(DEPTH):
        past = None
        if cache is not None:
            cache_k, cache_v, page_table, state_hgrn, state_conv = cache
            n_seq, n_pages = page_table.shape
            past_len = n_pages * cache_k.shape[2]
            past = dict(k=cache_k[l][page_table].reshape(n_seq, past_len, SB_HEADS, SB_HEAD_DIM),
                        v=cache_v[l][page_table].reshape(n_seq, past_len, SB_HEADS, SB_HEAD_DIM),
                        s=state_hgrn[l], conv=state_conv[l])
        h = _rmsnorm(x, W['g_pre_mix'][l])
        mix, (k_new, v_new, s_new, c_new) = _mixer(h, l, W, lb, past)
        x = x + _rmsnorm(mix, W['g_post_mix'][l])
        h = _rmsnorm(x, W['g_pre_ffn'][l])
        ff = jnp.square(jax.nn.relu(h @ W['w_ff1'][l])) @ W['w_ff2'][l]
        x = x + _rmsnorm(ff, W['g_post_ffn'][l])
        ks.append(k_new); vs.append(v_new); ss.append(s_new); cs.append(c_new)
    return x, jnp.stack(ks), jnp.stack(vs), jnp.stack(ss), jnp.stack(cs)


def setup_inputs(seed: int = 0) -> dict:
    key = jax.random.key(seed)
    ks = jax.random.split(key, 32)
    f32 = jnp.float32
    n_pages = PAST_LEN // PAGE_SIZE
    n_used = DEC_BATCH * n_pages
    n_phys = (5 * n_used + 3) // 4
    nrm = lambda k, shape, s: (jax.random.normal(k, shape, f32) * s)
    gain = lambda k, shape: 1.0 + 0.05 * jax.random.normal(k, shape, f32)
    perm = jax.random.permutation(ks[5], n_phys)[:n_used]
    sb_bias = jnp.linspace(SB_BIAS_HI, SB_BIAS_LO, SB_HEADS, dtype=f32)[None, :] + nrm(ks[24], (DEPTH, SB_HEADS), 0.1)
    return {
        'x_prompt': nrm(ks[0], (BATCH, SEQ, D_MODEL), 1.0),
        'x_sample': nrm(ks[1], (DEC_BATCH, DEC_SEQ, D_MODEL), 1.0),
        'cache_k': nrm(ks[2], (DEPTH, n_phys, PAGE_SIZE, SB_HEADS, SB_HEAD_DIM), 1.0),
        'cache_v': nrm(ks[3], (DEPTH, n_phys, PAGE_SIZE, SB_HEADS, SB_HEAD_DIM), 1.0),
        'page_table': perm.reshape(DEC_BATCH, n_pages).astype(jnp.int32),
        'state_hgrn': nrm(ks[4], (DEPTH, DEC_BATCH, A_HEADS, A_DK, A_DV), 0.5),
        'state_conv': nrm(ks[6], (DEPTH, DEC_BATCH, CONV_W - 1, C_CH), 0.5),
        'g_pre_mix': gain(ks[7], (DEPTH, D_MODEL)),
        'g_post_mix': gain(ks[8], (DEPTH, D_MODEL)),
        'g_pre_ffn': gain(ks[9], (DEPTH, D_MODEL)),
        'g_post_ffn': gain(ks[10], (DEPTH, D_MODEL)),
        'w_in': nrm(ks[11], (DEPTH, D_MODEL, IN_WIDTH), D_MODEL ** -0.5),
        'hgrn_lb': nrm(ks[12], (DEPTH, A_WIDTH_K), 1.0),
        'a_norm': gain(ks[13], (DEPTH, A_DV)),
        'w_pa': nrm(ks[14], (DEPTH, A_WIDTH_V, D_MODEL), A_WIDTH_V ** -0.5),
        'w_pb': nrm(ks[15], (DEPTH, SB_WIDTH, D_MODEL), SB_WIDTH ** -0.5),
        'sb_bias': sb_bias,
        'conv_w': nrm(ks[16], (DEPTH, CONV_W, C_CH), CONV_W ** -0.5),
        'conv_b': nrm(ks[17], (DEPTH, C_CH), 0.02),
        'ln_g': gain(ks[18], (DEPTH, C_CH)),
        'ln_b': nrm(ks[19], (DEPTH, C_CH), 0.02),
        'w_pc': nrm(ks[20], (DEPTH, C_CH, D_MODEL), C_CH ** -0.5),
        'w_o': nrm(ks[21], (DEPTH, D_MODEL, D_MODEL), D_MODEL ** -0.5),
        'w_ff1': nrm(ks[22], (DEPTH, D_MODEL, D_FF), D_MODEL ** -0.5),
        'w_ff2': nrm(ks[23], (DEPTH, D_FF, D_MODEL), D_FF ** -0.5),
    }


def reference(x_prompt, x_sample, cache_k, cache_v, page_table, state_hgrn, state_conv,
              g_pre_mix, g_post_mix, g_pre_ffn, g_post_ffn, w_in, hgrn_lb, a_norm, w_pa, w_pb, sb_bias,
              conv_w, conv_b, ln_g, ln_b, w_pc, w_o, w_ff1, w_ff2):
    W = dict(g_pre_mix=g_pre_mix, g_post_mix=g_post_mix, g_pre_ffn=g_pre_ffn, g_post_ffn=g_post_ffn,
             w_in=w_in, a_norm=a_norm, w_pa=w_pa, w_pb=w_pb, sb_bias=sb_bias, conv_w=conv_w, conv_b=conv_b,
             ln_g=ln_g, ln_b=ln_b, w_pc=w_pc, w_o=w_o, w_ff1=w_ff1, w_ff2=w_ff2)
    lb = jnp.cumsum(jax.nn.softmax(hgrn_lb.astype(jnp.float32), axis=0), axis=0)
    lb = lb - lb[0:1]
    y_prompt, k_p, v_p, s_p, c_p = _trunk(x_prompt, W, lb, None)
    y_sample, k_s, v_s, s_s, c_s = _trunk(x_sample, W, lb, (cache_k, cache_v, page_table, state_hgrn, state_conv))
    return (y_prompt, y_sample, k_p, v_p, s_p, c_p, k_s, v_s, s_s, c_s)
```

```python
import functools

import numpy as np
import jax
import jax.numpy as jnp
from jax import lax
from jax.experimental import pallas as pl
from jax.experimental.pallas import tpu as pltpu

F32 = jnp.float32
BF16 = jnp.bfloat16

D_MODEL = 1024
A_HEADS = 4
A_DK = 128
A_DV = 128
A_WIDTH = A_HEADS * A_DK
SB_HEADS = 8
SB_HEAD_DIM = 64
SB_WIDTH = SB_HEADS * SB_HEAD_DIM
C_CH = 512
CONV_W = 31
D_FF = 4 * D_MODEL
N_BRANCH = 3
EPS = 1e-6
PAGE_SIZE = 128
KV_OFFSET = 4 * A_WIDTH + SB_WIDTH

LANES = 128
SUBLANES = 8
VMEM_LIMIT_BYTES = 56 * 1024 * 1024

HGRN_ROWS = 128
SB_TILE = 256
CONV_HALO = 32

_NT = (((1,), (1,)), ((), ()))


def _params(*semantics):
    return pltpu.CompilerParams(dimension_semantics=semantics, vmem_limit_bytes=VMEM_LIMIT_BYTES)


def _resident(shape):
    nd = len(shape)
    return pl.BlockSpec(shape, lambda *_: (0,) * nd, pipeline_mode=pl.Buffered(1))


def _sigmoid(x):
    return 1.0 / (1.0 + jnp.exp(-x))


def _softplus(x):
    return jnp.maximum(x, 0.0) + jnp.log1p(jnp.exp(-jnp.abs(x)))


def _rms(x, g):
    return x * lax.rsqrt(jnp.mean(x * x, axis=-1, keepdims=True) + EPS) * g


def _split3(x):
    a = x.astype(BF16)
    r = x - a.astype(F32)
    b = r.astype(BF16)
    c = (r - b.astype(F32)).astype(BF16)
    return a, b, c


def _dot(a, b):
    return jnp.dot(a, b, preferred_element_type=F32)


def _in_proj_body(x_ref, g_ref, w_ref, lb_ref, *refs, layer0, transposed_kv):
    if transposed_kv:
        wkv_ref, qa_ref, lf_ref, ia_ref, ga_ref, qb_ref, u_ref, gt_ref, kt_ref, vt_ref, kth_ref, vth_ref = refs
    else:
        qa_ref, lf_ref, ia_ref, ga_ref, qb_ref, u_ref, gt_ref, kb_ref, vb_ref = refs
    h = _rms(x_ref[...], g_ref[...]).astype(BF16)

    def seg(lo, n):
        return _dot(h, w_ref[:, lo:lo + n])

    qa = seg(0, A_WIDTH)
    qa_ref[...] = qa * _sigmoid(qa)
    fa = seg(A_WIDTH, A_WIDTH)
    if layer0:
        lf_ref[...] = -_softplus(-fa)
    else:
        lb = lb_ref[...]
        lf_ref[...] = jnp.log(lb + (1.0 - lb) * _sigmoid(fa))
    ia_ref[...] = seg(2 * A_WIDTH, A_WIDTH)
    ga = seg(3 * A_WIDTH, A_WIDTH)
    ga_ref[...] = ga * _sigmoid(ga)
    qb_ref[...] = (seg(4 * A_WIDTH, SB_WIDTH) * (SB_HEAD_DIM ** -0.5)).astype(BF16)
    if transposed_kv:
        kv = lax.dot_general(wkv_ref[...], h, _NT, preferred_element_type=F32)
        kt_ref[...] = kv[0:SB_WIDTH]
        vt_ref[...] = kv[SB_WIDTH:2 * SB_WIDTH]
        kth_ref[...] = kv[0:SB_WIDTH].astype(BF16)
        vth_ref[...] = kv[SB_WIDTH:2 * SB_WIDTH].astype(BF16)
    else:
        kb_ref[...] = seg(KV_OFFSET, SB_WIDTH)
        vb_ref[...] = seg(KV_OFFSET + SB_WIDTH, SB_WIDTH)
    o = KV_OFFSET + 2 * SB_WIDTH
    u_ref[...] = seg(o, C_CH) * _sigmoid(seg(o + C_CH, C_CH))
    o += 2 * C_CH
    for j in range(N_BRANCH):
        gt_ref[:, j * D_MODEL:(j + 1) * D_MODEL] = _sigmoid(seg(o + j * D_MODEL, D_MODEL))


def _in_proj(x, g, w, lb, layer0, tm, batch=None):
    n = x.shape[0]
    in_width = w.shape[1]
    row = lambda width: pl.BlockSpec((tm, width), lambda i: (i, 0))
    widths = [A_WIDTH] * 4 + [SB_WIDTH, C_CH, N_BRANCH * D_MODEL]
    dtypes = [F32] * 4 + [BF16, F32, F32]
    in_specs = [row(D_MODEL), _resident((1, D_MODEL)), _resident((D_MODEL, in_width)), _resident((1, A_WIDTH))]
    out_specs = [row(wd) for wd in widths]
    out_shape = [jax.ShapeDtypeStruct((n, wd), dt) for wd, dt in zip(widths, dtypes)]
    args = [x, g, w, lb]
    if batch is None:
        out_specs += [row(SB_WIDTH)] * 2
        out_shape += [jax.ShapeDtypeStruct((n, SB_WIDTH), F32)] * 2
    else:
        nt = n // batch // tm
        in_specs.append(_resident((2 * SB_WIDTH, D_MODEL)))
        args.append(w[:, KV_OFFSET:KV_OFFSET + 2 * SB_WIDTH].T)
        out_specs += [pl.BlockSpec((None, SB_WIDTH, tm), lambda i: (i // nt, 0, i % nt))] * 2
        out_specs += [pl.BlockSpec((None, None, SB_WIDTH, tm), lambda i: (i // nt, i % nt, 0, 0))] * 2
        out_shape += [jax.ShapeDtypeStruct((batch, SB_WIDTH, nt * tm), F32)] * 2
        out_shape += [jax.ShapeDtypeStruct((batch, nt, SB_WIDTH, tm), BF16)] * 2
    return pl.pallas_call(
        functools.partial(_in_proj_body, layer0=layer0, transposed_kv=batch is not None),
        grid=(n // tm,),
        in_specs=in_specs,
        out_specs=out_specs,
        out_shape=out_shape,
        compiler_params=_params("parallel"),
        name="in_proj",
    )(*args)


def _level_matrix(rows, seg_len):
    r = np.arange(rows)[:, None]
    u = np.arange(rows)[None, :]
    mats = []
    m = seg_len // 2
    while m >= 1:
        start = (r // (2 * m)) * (2 * m)
        bound = start + m - 1
        is_q = (r - start) >= m
        wq = is_q & (u > bound) & (u <= r)
        wk = (~is_q) & (u > r) & (u <= bound)
        mats.append(wq | wk)
        m //= 2
    seg = (r // seg_len) == (u // seg_len)
    mats.append(seg & (u <= r))
    mats.append(seg)
    return np.concatenate(mats, axis=0).astype(np.float32)


def _num_levels(seg_len):
    return int(np.log2(seg_len))


def _level_exponents(w, g):
    g1, g2, g3 = _split3(g)
    return _dot(w, g1) + _dot(w, g2) + _dot(w, g3)


def _hgrn_intra(q, k, v, e_of, seg_len):
    rows = q.shape[0]
    row = lax.broadcasted_iota(jnp.int32, (rows, rows), 0)
    col = lax.broadcasted_iota(jnp.int32, (rows, rows), 1)
    rowq = lax.broadcasted_iota(jnp.int32, q.shape, 0)
    att = None
    m = seg_len // 2
    lev = 0
    while m >= 1:
        ee = jnp.exp(e_of(lev))
        is_q = (rowq & (2 * m - 1)) >= m
        qt = jnp.where(is_q, q * ee, 0.0).astype(BF16)
        kt = jnp.where(is_q, 0.0, k * ee).astype(BF16)
        a = lax.dot_general(qt, kt, _NT, preferred_element_type=F32)
        if 2 * m < rows:
            sh = int(np.log2(2 * m))
            a = jnp.where((row >> sh) == (col >> sh), a, 0.0)
        att = a if att is None else att + a
        m //= 2
        lev += 1
    d = jnp.sum(q * k, axis=1, keepdims=True)
    return _dot(att.astype(BF16), v.astype(BF16)) + d * v


def _hgrn_finish(o, an, ga):
    return (_rms(o, an) * ga).astype(BF16)


def _hgrn_prompt_body(q_ref, lf_ref, v_ref, ga_ref, w_ref, an_ref, o_ref, s_ref, st_scr, e_scr):
    c = pl.program_id(1)
    rows = HGRN_ROWS
    nlev = _num_levels(rows)

    @pl.when(c == 0)
    def _():
        st_scr[...] = jnp.zeros_like(st_scr)

    g = lf_ref[...]
    e_scr[...] = _level_exponents(w_ref[...], g)

    for h in range(A_HEADS):
        sl = slice(h * A_DK, (h + 1) * A_DK)
        q = q_ref[:, sl]
        v = v_ref[:, sl]
        k = 1.0 - jnp.exp(g[:, sl])
        o = _hgrn_intra(q, k, v, lambda lev: e_scr[lev * rows:(lev + 1) * rows, sl], rows)
        b = e_scr[nlev * rows:(nlev + 1) * rows, sl]
        st = st_scr[h]
        o = o + lax.dot_general((q * jnp.exp(b)).astype(BF16), st.astype(BF16), _NT, preferred_element_type=F32)
        bl = b[rows - 1:rows, :]
        kk = (k * jnp.exp(bl - b)).astype(BF16)
        st_scr[h] = st * jnp.exp(bl) + _dot(v.T.astype(BF16), kk)
        o_ref[:, sl] = _hgrn_finish(o, an_ref[...], ga_ref[:, sl])

    @pl.when(c == pl.num_programs(1) - 1)
    def _():
        for h in range(A_HEADS):
            s_ref[h] = st_scr[h].T


def _hgrn_prompt(qa, lf, ia, ga, an, batch):
    n = qa.shape[0]
    rows = HGRN_ROWS
    nc = n // batch // rows
    w = jnp.asarray(_level_matrix(rows, rows), BF16)
    blk = pl.BlockSpec((rows, A_WIDTH), lambda b, c: (b * nc + c, 0))
    return pl.pallas_call(
        _hgrn_prompt_body,
        grid=(batch, nc),
        in_specs=[blk, blk, blk, blk, _resident(w.shape), _resident((1, A_DV))],
        out_specs=[blk, pl.BlockSpec((None, A_HEADS, A_DK, A_DV), lambda b, c: (b, 0, 0, 0))],
        out_shape=[jax.ShapeDtypeStruct((n, A_WIDTH), BF16),
                   jax.ShapeDtypeStruct((batch, A_HEADS, A_DK, A_DV), F32)],
        scratch_shapes=[pltpu.VMEM((A_HEADS, A_DV, A_DK), F32), pltpu.VMEM(w.shape[:1] + (A_WIDTH,), F32)],
        compiler_params=_params("parallel", "arbitrary"),
        name="hgrn_prompt",
    )(qa, lf, ia, ga, w, an)


def _hgrn_decode_body(q_ref, lf_ref, v_ref, ga_ref, s_ref, w_ref, an_ref, o_ref, so_ref, *, seg_len):
    rows = q_ref.shape[0]
    nseq = rows // seg_len
    nlev = _num_levels(seg_len)
    g = lf_ref[...]
    e = _level_exponents(w_ref[...], g)
    q = q_ref[...]
    v = v_ref[...]
    k = 1.0 - jnp.exp(g)
    o = _hgrn_intra(q, k, v, lambda lev: e[lev * rows:(lev + 1) * rows, :], seg_len)
    b = e[nlev * rows:(nlev + 1) * rows, :]
    btot = e[(nlev + 1) * rows:(nlev + 2) * rows, :]
    qh = q * jnp.exp(b)
    kkt = (k * jnp.exp(btot - b)).T
    ebt = jnp.exp(btot).T
    vh = v.astype(BF16)
    row = lax.broadcasted_iota(jnp.int32, (rows, A_DK), 0)
    col = lax.broadcasted_iota(jnp.int32, (A_DK, rows), 1)
    sh = int(np.log2(seg_len))
    for s in range(nseq):
        st = s_ref[s]
        o = o + _dot(jnp.where((row >> sh) == s, qh, 0.0).astype(BF16), st.astype(BF16))
        decay = jnp.broadcast_to(ebt[:, s * seg_len:s * seg_len + 1], (A_DK, A_DV))
        so_ref[s] = decay * st + _dot(jnp.where((col >> sh) == s, kkt, 0.0).astype(BF16), vh)
    o_ref[...] = _hgrn_finish(o, an_ref[...], ga_ref[...])


def _hgrn_decode(qa, lf, ia, ga, state, layer, an, seg_len):
    n = qa.shape[0]
    rows = HGRN_ROWS
    nseq = rows // seg_len
    w = jnp.asarray(_level_matrix(rows, seg_len), BF16)
    blk = pl.BlockSpec((rows, A_DK), lambda i, h: (i, h))
    return pl.pallas_call(
        functools.partial(_hgrn_decode_body, seg_len=seg_len),
        grid=(n // rows, A_HEADS),
        in_specs=[blk, blk, blk, blk,
                  pl.BlockSpec((None, nseq, None, A_DK, A_DV), lambda i, h: (layer, i, h, 0, 0)),
                  _resident(w.shape), _resident((1, A_DV))],
        out_specs=[blk, pl.BlockSpec((nseq, None, A_DK, A_DV), lambda i, h: (i, h, 0, 0))],
        out_shape=[jax.ShapeDtypeStruct((n, A_WIDTH), BF16), jax.ShapeDtypeStruct(state.shape[1:], F32)],
        compiler_params=_params("parallel", "parallel"),
        name="hgrn_decode",
    )(qa, lf, ia, ga, state, w, an)


def _sb_tile(qh, k, v, bias, run, tri, mask, kv_transposed):
    if kv_transposed:
        z = _dot(qh, k) + bias
    else:
        z = lax.dot_general(qh, k, _NT, preferred_element_type=F32) + bias
    sp = _softplus(z)
    lr = -sp
    if mask is not None:
        lr = jnp.where(mask, lr, 0.0)
    hi = lr.astype(BF16)
    lo = (lr - hi.astype(F32)).astype(BF16)
    tail = _dot(hi, tri) + _dot(lo, tri)
    a = jnp.exp(z - sp + tail + run)
    if mask is not None:
        a = jnp.where(mask, a, 0.0)
    a = a.astype(BF16)
    if kv_transposed:
        pv = lax.dot_general(a, v, _NT, preferred_element_type=F32)
    else:
        pv = _dot(a, v)
    return pv, run + jnp.sum(lr, axis=1, keepdims=True)


def _sb_prompt_body(bias_ref, q_ref, k_ref, v_ref, tri_ref, o_ref, acc_ref):
    pair = pl.program_id(1)
    i = pl.program_id(2)
    t = SB_TILE
    q = q_ref[...]
    tri = tri_ref[...]
    lane = lax.broadcasted_iota(jnp.int32, q.shape, 1)
    row = lax.broadcasted_iota(jnp.int32, (t, t), 0)
    col = lax.broadcasted_iota(jnp.int32, (t, t), 1)
    causal = col < row
    qhs = [jnp.where((lane < SB_HEAD_DIM) == (hh == 0), q, jnp.zeros_like(q)) for hh in range(2)]
    biases = [bias_ref[2 * pair + hh] for hh in range(2)]

    kd = k_ref[i]
    vd = v_ref[i]
    runs = []
    for hh in range(2):
        pv, run = _sb_tile(qhs[hh], kd, vd, biases[hh], jnp.zeros((t, 1), F32), tri, causal, True)
        acc_ref[hh] = pv
        runs.append(run)

    def step(n, carry):
        j = i - 1 - n
        kt = k_ref[j]
        vt = v_ref[j]
        out = []
        for hh in range(2):
            pv, run = _sb_tile(qhs[hh], kt, vt, biases[hh], carry[hh], tri, None, True)
            acc_ref[hh] += pv
            out.append(run)
        return tuple(out)

    lax.fori_loop(0, i, step, tuple(runs))
    o_ref[...] = jnp.where(lane < SB_HEAD_DIM, acc_ref[0], acc_ref[1]).astype(BF16)


def _strict_lower(n):
    r = np.arange(n)
    return jnp.asarray((r[:, None] > r[None, :]).astype(np.float32), BF16)


def _sb_prompt(qb, kth, vth, bias):
    n = qb.shape[0]
    batch, nq, _, t = kth.shape
    assert t == SB_TILE
    tri = _strict_lower(t)
    qblk = pl.BlockSpec((t, LANES), lambda b, p, i: (b * nq + i, p))
    kvblk = pl.BlockSpec((None, nq, LANES, t), lambda b, p, i: (b, 0, p, 0))
    return pl.pallas_call(
        _sb_prompt_body,
        grid=(batch, SB_WIDTH // LANES, nq),
        in_specs=[pl.BlockSpec(memory_space=pltpu.SMEM), qblk, kvblk, kvblk, _resident((t, t))],
        out_specs=qblk,
        out_shape=jax.ShapeDtypeStruct((n, SB_WIDTH), BF16),
        scratch_shapes=[pltpu.VMEM((2, t, LANES), F32)],
        compiler_params=_params("parallel", "parallel", "arbitrary"),
        name="sb_prompt",
    )(bias, qb, kth, vth, tri)


def _sb_decode_body(pt_ref, q_ref, kn_ref, vn_ref, bias_ref, tri_ref, *refs, n_pages, n_new):
    kp = refs[:n_pages]
    vp = refs[n_pages:2 * n_pages]
    o_ref = refs[2 * n_pages]
    kpad, vpad = refs[2 * n_pages + 1:]
    del pt_ref
    rows = n_new * SB_HEADS
    q = q_ref[...]
    qbd = jnp.concatenate([jnp.broadcast_to(q[t:t + 1, :], (SB_HEADS, SB_WIDTH)) for t in range(n_new)], axis=0)
    row = lax.broadcasted_iota(jnp.int32, (rows, SB_WIDTH), 0)
    lane = lax.broadcasted_iota(jnp.int32, (rows, SB_WIDTH), 1)
    own = (lane // SB_HEAD_DIM) == (row % SB_HEADS)
    qbd = jnp.where(own, qbd, 0.0).astype(BF16)
    bias = bias_ref[...]
    tri = tri_ref[...]

    kpad[...] = jnp.zeros_like(kpad)
    vpad[...] = jnp.zeros_like(vpad)
    kpad[0:n_new, :] = kn_ref[...]
    vpad[0:n_new, :] = vn_ref[...]
    mrow = lax.broadcasted_iota(jnp.int32, (rows, PAGE_SIZE), 0)
    mcol = lax.broadcasted_iota(jnp.int32, (rows, PAGE_SIZE), 1)
    new_mask = mcol < (mrow // SB_HEADS)
    acc, run = _sb_tile(qbd, kpad[...].astype(BF16), vpad[...].astype(BF16), bias, jnp.zeros((rows, 1), F32), tri,
                        new_mask, False)
    for p in range(n_pages - 1, -1, -1):
        kt = kp[p][...].reshape(SB_WIDTH, PAGE_SIZE).astype(BF16)
        vt = vp[p][...].reshape(SB_WIDTH, PAGE_SIZE).astype(BF16)
        pv, run = _sb_tile(qbd, kt, vt, bias, run, tri, None, True)
        acc = acc + pv
    acc = jnp.where(own, acc, 0.0)
    o_ref[...] = jnp.sum(acc.reshape(n_new, SB_HEADS, SB_WIDTH), axis=1)


def _sb_decode(qb, kb, vb, cache_kt, cache_vt, layer, page_table, bias_rows):
    n_seq, n_new, _ = qb.shape
    n_pages = page_table.shape[1]
    rows = n_new * SB_HEADS
    tri = _strict_lower(PAGE_SIZE)
    tok = pl.BlockSpec((None, n_new, SB_WIDTH), lambda s, pt: (s, 0, 0))

    def page(p):
        return pl.BlockSpec((None, None, SB_HEADS, SB_HEAD_DIM, PAGE_SIZE), lambda s, pt: (layer, pt[s, p], 0, 0, 0))

    grid_spec = pltpu.PrefetchScalarGridSpec(
        num_scalar_prefetch=1,
        grid=(n_seq,),
        in_specs=[tok, tok, tok, _resident((rows, PAGE_SIZE)), _resident((PAGE_SIZE, PAGE_SIZE))]
                 + [page(p) for p in range(n_pages)] * 2,
        out_specs=tok,
        scratch_shapes=[pltpu.VMEM((PAGE_SIZE, SB_WIDTH), F32), pltpu.VMEM((PAGE_SIZE, SB_WIDTH), F32)],
    )
    return pl.pallas_call(
        functools.partial(_sb_decode_body, n_pages=n_pages, n_new=n_new),
        grid_spec=grid_spec,
        out_shape=jax.ShapeDtypeStruct((n_seq, n_new, SB_WIDTH), F32),
        compiler_params=_params("arbitrary"),
        name="sb_decode",
    )(page_table, qb, kb, vb, bias_rows, tri, *([cache_kt] * n_pages), *([cache_vt] * n_pages))


def _conv_finish(y, cb, lg, lb):
    y = y + cb
    mu = jnp.mean(y, axis=-1, keepdims=True)
    yc = y - mu
    var = jnp.mean(yc * yc, axis=-1, keepdims=True)
    y = yc * lax.rsqrt(var + EPS) * lg + lb
    return y * _sigmoid(y)


def _conv_prompt_body(u_ref, prev_ref, w_ref, cb_ref, lg_ref, lb_ref, o_ref, buf, *, tile, chunk):
    i = pl.program_id(1)
    prev = prev_ref[...]
    buf[0:CONV_HALO, :] = jnp.where(i > 0, prev, jnp.zeros_like(prev))
    buf[CONV_HALO:CONV_HALO + tile, :] = u_ref[...]
    base = CONV_HALO - (CONV_W - 1)
    for r0 in range(0, tile, chunk):
        acc = jnp.zeros((chunk, C_CH), F32)
        for t in range(CONV_W):
            acc = acc + w_ref[t:t + 1, :] * buf[r0 + base + t:r0 + base + t + chunk, :]
        o_ref[r0:r0 + chunk, :] = _conv_finish(acc, cb_ref[...], lg_ref[...], lb_ref[...]).astype(BF16)


def _conv_prompt(u, w, cb, lg, lb, batch, tile=256, chunk=32):
    n = u.shape[0]
    nt = n // batch // tile
    per = tile // CONV_HALO
    cur = pl.BlockSpec((tile, C_CH), lambda b, i: (b * nt + i, 0))
    prev = pl.BlockSpec((CONV_HALO, C_CH), lambda b, i: (jnp.maximum((b * nt + i) * per - 1, 0), 0))
    vec = _resident((1, C_CH))
    return pl.pallas_call(
        functools.partial(_conv_prompt_body, tile=tile, chunk=chunk),
        grid=(batch, nt),
        in_specs=[cur, prev, _resident((CONV_W, C_CH)), vec, vec, vec],
        out_specs=cur,
        out_shape=jax.ShapeDtypeStruct((n, C_CH), BF16),
        scratch_shapes=[pltpu.VMEM((CONV_HALO + tile, C_CH), F32)],
        compiler_params=_params("parallel", "parallel"),
        name="conv_prompt",
    )(u, u, w, cb, lg, lb)


def _conv_decode_body(u_ref, st_ref, w_ref, cb_ref, lg_ref, lb_ref, o_ref, so_ref, *, n_new):
    ctx = CONV_W - 1
    full = lambda j: st_ref[j] if j < ctx else u_ref[j - ctx]
    for t in range(n_new):
        acc = w_ref[0:1, :] * full(t)
        for j in range(1, CONV_W):
            acc = acc + w_ref[j:j + 1, :] * full(t + j)
        o_ref[t] = _conv_finish(acc, cb_ref[...], lg_ref[...], lb_ref[...])
    for j in range(ctx):
        so_ref[j] = full(j + n_new)


def _conv_decode(u, state, layer, w, cb, lg, lb, per_step=16):
    n_new, n_seq, _ = u.shape
    ctx = CONV_W - 1
    ublk = pl.BlockSpec((n_new, per_step, C_CH), lambda i: (0, i, 0))
    vec = _resident((1, C_CH))
    return pl.pallas_call(
        functools.partial(_conv_decode_body, n_new=n_new),
        grid=(n_seq // per_step,),
        in_specs=[ublk, pl.BlockSpec((None, ctx, per_step, C_CH), lambda i: (layer, 0, i, 0)),
                  _resident((CONV_W, C_CH)), vec, vec, vec],
        out_specs=[ublk, pl.BlockSpec((ctx, per_step, C_CH), lambda i: (0, i, 0))],
        out_shape=[jax.ShapeDtypeStruct((n_new, n_seq, C_CH), F32), jax.ShapeDtypeStruct((ctx, n_seq, C_CH), F32)],
        compiler_params=_params("parallel"),
        name="conv_decode",
    )(u, state, w, cb, lg, lb)


def _mix_out_body(x_ref, oa_ref, ob_ref, oc_ref, gt_ref, wpa_ref, wpb_ref, wpc_ref, wo_ref, g_ref, y_ref):
    m = (gt_ref[:, 0:D_MODEL] * _dot(oa_ref[...], wpa_ref[...])
         + gt_ref[:, D_MODEL:2 * D_MODEL] * _dot(ob_ref[...], wpb_ref[...])
         + gt_ref[:, 2 * D_MODEL:3 * D_MODEL] * _dot(oc_ref[...], wpc_ref[...]))
    out = _dot(m.astype(BF16), wo_ref[...])
    y_ref[...] = x_ref[...] + _rms(out, g_ref[...])


def _mix_out(x, oa, ob, oc, gt, wpa, wpb, wpc, wo, g, tm):
    n = x.shape[0]
    row = lambda width: pl.BlockSpec((tm, width), lambda i: (i, 0))
    return pl.pallas_call(
        _mix_out_body,
        grid=(n // tm,),
        in_specs=[row(D_MODEL), row(A_WIDTH), row(SB_WIDTH), row(C_CH), row(N_BRANCH * D_MODEL),
                  _resident(wpa.shape), _resident(wpb.shape), _resident(wpc.shape), _resident(wo.shape),
                  _resident((1, D_MODEL))],
        out_specs=row(D_MODEL),
        out_shape=jax.ShapeDtypeStruct((n, D_MODEL), F32),
        compiler_params=_params("parallel"),
        name="mix_out",
    )(x, oa, ob, oc, gt, wpa, wpb, wpc, wo, g)


def _ffn_body(x_ref, g1_ref, w1_ref, w2_ref, g2_ref, y_ref, *, chunk):
    x = x_ref[...]
    h = _rms(x, g1_ref[...]).astype(BF16)
    acc = None
    for c0 in range(0, D_FF, chunk):
        a = jnp.maximum(_dot(h, w1_ref[:, c0:c0 + chunk]), 0.0)
        part = _dot((a * a).astype(BF16), w2_ref[c0:c0 + chunk, :])
        acc = part if acc is None else acc + part
    y_ref[...] = x + _rms(acc, g2_ref[...])


def _ffn(x, g1, w1, w2, g2, tm, chunk=1024):
    n = x.shape[0]
    row = pl.BlockSpec((tm, D_MODEL), lambda i: (i, 0))
    vec = _resident((1, D_MODEL))
    return pl.pallas_call(
        functools.partial(_ffn_body, chunk=chunk),
        grid=(n // tm,),
        in_specs=[row, vec, _resident(w1.shape), _resident(w2.shape), vec],
        out_specs=row,
        out_shape=jax.ShapeDtypeStruct((n, D_MODEL), F32),
        compiler_params=_params("parallel"),
        name="ffn",
    )(x, g1, w1, w2, g2)


def _trunk(x3, wts, lb, cache):
    batch, seq, _ = x3.shape
    n = batch * seq
    x = x3.reshape(n, D_MODEL)
    tm = SB_TILE
    depth = wts["w_in"].shape[0]
    ks, vs, ss, cs = [], [], [], []
    for l in range(depth):
        vec = lambda name: wts[name][l].reshape(1, -1)
        conv_args = (wts["conv_w"][l], vec("conv_b"), vec("ln_g"), vec("ln_b"))
        proj_args = (x, vec("g_pre_mix"), wts["w_in"][l], lb[l].reshape(1, -1), l == 0, tm)
        if cache is None:
            qa, lf, ia, ga, qb, u, gt, kt, vt, kth, vth = _in_proj(*proj_args, batch=batch)
            oa, s_new = _hgrn_prompt(qa, lf, ia, ga, vec("a_norm"), batch)
            ob = _sb_prompt(qb, kth, vth, wts["sb_bias"][l])
            oc = _conv_prompt(u, *conv_args, batch)
            c_new = u.reshape(batch, seq, C_CH)[:, seq - (CONV_W - 1):]
            k_new, v_new = (a.reshape(batch, SB_HEADS, SB_HEAD_DIM, seq).transpose(0, 3, 1, 2) for a in (kt, vt))
        else:
            cache_kt, cache_vt, page_table, state_hgrn, state_conv_t = cache
            qa, lf, ia, ga, qb, u, gt, kb, vb = _in_proj(*proj_args)
            oa, s_new = _hgrn_decode(qa, lf, ia, ga, state_hgrn, l, vec("a_norm"), seq)
            bias_rows = jnp.broadcast_to(jnp.tile(wts["sb_bias"][l], seq)[:, None], (seq * SB_HEADS, PAGE_SIZE))
            tok = lambda a: a.reshape(batch, seq, -1)
            ob = _sb_decode(tok(qb).astype(F32), tok(kb), tok(vb), cache_kt, cache_vt, l, page_table, bias_rows)
            ob = ob.reshape(n, SB_WIDTH).astype(BF16)
            oc, c_new = _conv_decode(tok(u).transpose(1, 0, 2), state_conv_t, l, *conv_args)
            oc = oc.transpose(1, 0, 2).reshape(n, C_CH).astype(BF16)
            c_new = c_new.transpose(1, 0, 2)
            k_new, v_new = (a.reshape(batch, seq, SB_HEADS, SB_HEAD_DIM) for a in (kb, vb))
        x = _mix_out(x, oa, ob, oc, gt, wts["w_pa"][l], wts["w_pb"][l], wts["w_pc"][l], wts["w_o"][l],
                     vec("g_post_mix"), tm)
        x = _ffn(x, vec("g_pre_ffn"), wts["w_ff1"][l], wts["w_ff2"][l], vec("g_post_ffn"), tm)
        ks.append(k_new)
        vs.append(v_new)
        ss.append(s_new)
        cs.append(c_new)
    return x.reshape(batch, seq, D_MODEL), jnp.stack(ks), jnp.stack(vs), jnp.stack(ss), jnp.stack(cs)


def kernel(x_prompt, x_sample, cache_k, cache_v, page_table, state_hgrn, state_conv, g_pre_mix, g_post_mix,
           g_pre_ffn, g_post_ffn, w_in, hgrn_lb, a_norm, w_pa, w_pb, sb_bias, conv_w, conv_b, ln_g, ln_b, w_pc,
           w_o, w_ff1, w_ff2):
    wts = dict(g_pre_mix=g_pre_mix, g_post_mix=g_post_mix, g_pre_ffn=g_pre_ffn, g_post_ffn=g_post_ffn,
               a_norm=a_norm, sb_bias=sb_bias, conv_w=conv_w, conv_b=conv_b, ln_g=ln_g, ln_b=ln_b,
               w_in=w_in.astype(BF16), w_pa=w_pa.astype(BF16), w_pb=w_pb.astype(BF16), w_pc=w_pc.astype(BF16),
               w_o=w_o.astype(BF16), w_ff1=w_ff1.astype(BF16), w_ff2=w_ff2.astype(BF16))
    lb = jnp.cumsum(jax.nn.softmax(hgrn_lb.astype(F32), axis=0), axis=0)
    lb = lb - lb[0:1]
    y_p, k_p, v_p, s_p, c_p = _trunk(x_prompt, wts, lb, None)
    cache = (cache_k.transpose(0, 1, 3, 4, 2), cache_v.transpose(0, 1, 3, 4, 2), page_table, state_hgrn,
             state_conv.transpose(0, 2, 1, 3))
    y_s, k_s, v_s, s_s, c_s = _trunk(x_sample, wts, lb, cache)
    return (y_p, y_s, k_p, v_p, s_p, c_p, k_s, v_s, s_s, c_s)
```

```python
import functools

import numpy as np
import jax
import jax.numpy as jnp
from jax import lax
from jax.experimental import pallas as pl
from jax.experimental.pallas import tpu as pltpu

F32 = jnp.float32
BF16 = jnp.bfloat16

D_MODEL = 1024
A_HEADS = 4
A_DK = 128
A_DV = 128
A_WIDTH = A_HEADS * A_DK
SB_HEADS = 8
SB_HEAD_DIM = 64
SB_WIDTH = SB_HEADS * SB_HEAD_DIM
C_CH = 512
CONV_W = 31
D_FF = 4 * D_MODEL
N_BRANCH = 3
EPS = 1e-6
PAGE_SIZE = 128
KV_OFFSET = 4 * A_WIDTH + SB_WIDTH
LOG2E = 1.4426950408889634
SB_QSCALE = -LOG2E * SB_HEAD_DIM ** -0.5

LANES = 128
SUBLANES = 8
VMEM_LIMIT_BYTES = 56 * 1024 * 1024

HGRN_ROWS = 128
SB_TILE = 256
CONV_HALO = 32

_NT = (((1,), (1,)), ((), ()))


def _params(*semantics):
    return pltpu.CompilerParams(dimension_semantics=semantics, vmem_limit_bytes=VMEM_LIMIT_BYTES)


def _resident(shape):
    nd = len(shape)
    return pl.BlockSpec(shape, lambda *_: (0,) * nd, pipeline_mode=pl.Buffered(1))


def _sigmoid(x):
    return 1.0 / (1.0 + jnp.exp(-x))


def _softplus(x):
    return jnp.maximum(x, 0.0) + jnp.log1p(jnp.exp(-jnp.abs(x)))


def _rms(x, g):
    return x * lax.rsqrt(jnp.mean(x * x, axis=-1, keepdims=True) + EPS) * g


def _dot(a, b):
    return jnp.dot(a, b, preferred_element_type=F32)


def _in_proj_body(x_ref, g_ref, w_ref, lb_ref, *refs, layer0, transposed_kv):
    if transposed_kv:
        wkv_ref, qa_ref, lf_ref, ia_ref, ga_ref, qb_ref, u_ref, gt_ref, kt_ref, vt_ref, kth_ref, vth_ref = refs
    else:
        qa_ref, lf_ref, ia_ref, ga_ref, qb_ref, u_ref, gt_ref, kb_ref, vb_ref = refs
    h = _rms(x_ref[...], g_ref[...]).astype(BF16)

    def seg(lo, n):
        return _dot(h, w_ref[:, lo:lo + n])

    qa = seg(0, A_WIDTH)
    qa_ref[...] = qa * _sigmoid(qa)
    fa = seg(A_WIDTH, A_WIDTH)
    if layer0:
        lf_ref[...] = -_softplus(-fa)
    else:
        lb = lb_ref[...]
        lf_ref[...] = jnp.log(lb + (1.0 - lb) * _sigmoid(fa))
    ia_ref[...] = seg(2 * A_WIDTH, A_WIDTH)
    ga = seg(3 * A_WIDTH, A_WIDTH)
    ga_ref[...] = ga * _sigmoid(ga)
    qb_ref[...] = (seg(4 * A_WIDTH, SB_WIDTH) * SB_QSCALE).astype(BF16)
    if transposed_kv:
        kv = lax.dot_general(wkv_ref[...], h, _NT, preferred_element_type=F32)
        kt_ref[...] = kv[0:SB_WIDTH]
        vt_ref[...] = kv[SB_WIDTH:2 * SB_WIDTH]
        kth_ref[...] = kv[0:SB_WIDTH].astype(BF16)
        vth_ref[...] = kv[SB_WIDTH:2 * SB_WIDTH].astype(BF16)
    else:
        kb_ref[...] = seg(KV_OFFSET, SB_WIDTH)
        vb_ref[...] = seg(KV_OFFSET + SB_WIDTH, SB_WIDTH)
    o = KV_OFFSET + 2 * SB_WIDTH
    u_ref[...] = seg(o, C_CH) * _sigmoid(seg(o + C_CH, C_CH))
    o += 2 * C_CH
    for j in range(N_BRANCH):
        gt_ref[:, j * D_MODEL:(j + 1) * D_MODEL] = _sigmoid(seg(o + j * D_MODEL, D_MODEL))


def _in_proj(x, g, w, lb, layer0, tm, batch=None):
    n = x.shape[0]
    in_width = w.shape[1]
    row = lambda width: pl.BlockSpec((tm, width), lambda i: (i, 0))
    widths = [A_WIDTH] * 4 + [SB_WIDTH, C_CH, N_BRANCH * D_MODEL]
    dtypes = [F32] * 4 + [BF16, F32, F32]
    in_specs = [row(D_MODEL), _resident((1, D_MODEL)), _resident((D_MODEL, in_width)), _resident((1, A_WIDTH))]
    out_specs = [row(wd) for wd in widths]
    out_shape = [jax.ShapeDtypeStruct((n, wd), dt) for wd, dt in zip(widths, dtypes)]
    args = [x, g, w, lb]
    if batch is None:
        out_specs += [row(SB_WIDTH)] * 2
        out_shape += [jax.ShapeDtypeStruct((n, SB_WIDTH), F32)] * 2
    else:
        nt = n // batch // tm
        in_specs.append(_resident((2 * SB_WIDTH, D_MODEL)))
        args.append(w[:, KV_OFFSET:KV_OFFSET + 2 * SB_WIDTH].T)
        out_specs += [pl.BlockSpec((None, SB_WIDTH, tm), lambda i: (i // nt, 0, i % nt))] * 2
        out_specs += [pl.BlockSpec((None, None, SB_WIDTH, tm), lambda i: (i // nt, i % nt, 0, 0))] * 2
        out_shape += [jax.ShapeDtypeStruct((batch, SB_WIDTH, nt * tm), F32)] * 2
        out_shape += [jax.ShapeDtypeStruct((batch, nt, SB_WIDTH, tm), BF16)] * 2
    return pl.pallas_call(
        functools.partial(_in_proj_body, layer0=layer0, transposed_kv=batch is not None),
        grid=(n // tm,),
        in_specs=in_specs,
        out_specs=out_specs,
        out_shape=out_shape,
        compiler_params=_params("parallel"),
        name="in_proj",
    )(*args)


def _level_matrix(rows, seg_len, with_total):
    r = np.arange(rows)[:, None]
    u = np.arange(rows)[None, :]
    mats = []
    m = seg_len // 2
    while m >= 1:
        start = (r // (2 * m)) * (2 * m)
        bound = start + m - 1
        is_q = (r - start) >= m
        wq = is_q & (u > bound) & (u <= r)
        wk = (~is_q) & (u > r) & (u <= bound)
        mats.append(wq | wk)
        m //= 2
    seg = (r // seg_len) == (u // seg_len)
    mats.append(seg & (u <= r))
    if with_total:
        mats.append(seg)
    return np.concatenate(mats, axis=0).astype(np.float32)


def _num_levels(seg_len):
    return int(np.log2(seg_len))


def _level_exponents(w, g):
    hi = g.astype(BF16)
    lo = (g - hi.astype(F32)).astype(BF16)
    return _dot(w, hi) + _dot(w, lo)


def _hgrn_intra(q, k, v, e_of, seg_len):
    rows = q.shape[0]
    row = lax.broadcasted_iota(jnp.int32, (rows, rows), 0)
    col = lax.broadcasted_iota(jnp.int32, (rows, rows), 1)
    rowq = lax.broadcasted_iota(jnp.int32, q.shape, 0)
    att = None
    m = seg_len // 2
    lev = 0
    while m >= 1:
        ee = jnp.exp(e_of(lev))
        is_q = (rowq & (2 * m - 1)) >= m
        qt = jnp.where(is_q, q * ee, 0.0).astype(BF16)
        kt = jnp.where(is_q, 0.0, k * ee).astype(BF16)
        a = lax.dot_general(qt, kt, _NT, preferred_element_type=F32)
        if 2 * m < rows:
            sh = int(np.log2(2 * m))
            a = jnp.where((row >> sh) == (col >> sh), a, 0.0)
        att = a if att is None else att + a
        m //= 2
        lev += 1
    d = jnp.sum(q * k, axis=1, keepdims=True)
    return _dot(att.astype(BF16), v.astype(BF16)) + d * v


def _hgrn_finish(o, an, ga):
    return (_rms(o, an) * ga).astype(BF16)


def _hgrn_prompt_body(q_ref, lf_ref, v_ref, ga_ref, w_ref, an_ref, o_ref, s_ref, st_scr, e_scr):
    c = pl.program_id(1)
    rows = HGRN_ROWS
    nlev = _num_levels(rows)

    @pl.when(c == 0)
    def _():
        st_scr[...] = jnp.zeros_like(st_scr)

    g = lf_ref[...]
    e_scr[...] = _level_exponents(w_ref[...], g)

    for h in range(A_HEADS):
        sl = slice(h * A_DK, (h + 1) * A_DK)
        q = q_ref[:, sl]
        v = v_ref[:, sl]
        k = 1.0 - jnp.exp(g[:, sl])
        o = _hgrn_intra(q, k, v, lambda lev: e_scr[lev * rows:(lev + 1) * rows, sl], rows)
        b = e_scr[nlev * rows:(nlev + 1) * rows, sl]
        st = st_scr[h]
        o = o + lax.dot_general((q * jnp.exp(b)).astype(BF16), st.astype(BF16), _NT, preferred_element_type=F32)
        bl = b[rows - 1:rows, :]
        kk = (k * jnp.exp(bl - b)).astype(BF16)
        st_scr[h] = st * jnp.exp(bl) + _dot(v.T.astype(BF16), kk)
        o_ref[:, sl] = _hgrn_finish(o, an_ref[...], ga_ref[:, sl])

    @pl.when(c == pl.num_programs(1) - 1)
    def _():
        for h in range(A_HEADS):
            s_ref[h] = st_scr[h].T


def _hgrn_prompt(qa, lf, ia, ga, an, batch):
    n = qa.shape[0]
    rows = HGRN_ROWS
    nc = n // batch // rows
    w = jnp.asarray(_level_matrix(rows, rows, False), BF16)
    blk = pl.BlockSpec((rows, A_WIDTH), lambda b, c: (b * nc + c, 0))
    return pl.pallas_call(
        _hgrn_prompt_body,
        grid=(batch, nc),
        in_specs=[blk, blk, blk, blk, _resident(w.shape), _resident((1, A_DV))],
        out_specs=[blk, pl.BlockSpec((None, A_HEADS, A_DK, A_DV), lambda b, c: (b, 0, 0, 0))],
        out_shape=[jax.ShapeDtypeStruct((n, A_WIDTH), BF16),
                   jax.ShapeDtypeStruct((batch, A_HEADS, A_DK, A_DV), F32)],
        scratch_shapes=[pltpu.VMEM((A_HEADS, A_DV, A_DK), F32), pltpu.VMEM(w.shape[:1] + (A_WIDTH,), F32)],
        compiler_params=_params("parallel", "arbitrary"),
        name="hgrn_prompt",
    )(qa, lf, ia, ga, w, an)


def _hgrn_decode_body(q_ref, lf_ref, v_ref, ga_ref, s_ref, w_ref, an_ref, o_ref, so_ref, *, seg_len):
    rows = q_ref.shape[0]
    nseq = rows // seg_len
    nlev = _num_levels(seg_len)
    g = lf_ref[...]
    e = _level_exponents(w_ref[...], g)
    q = q_ref[...]
    v = v_ref[...]
    k = 1.0 - jnp.exp(g)
    o = _hgrn_intra(q, k, v, lambda lev: e[lev * rows:(lev + 1) * rows, :], seg_len)
    b = e[nlev * rows:(nlev + 1) * rows, :]
    btot = e[(nlev + 1) * rows:(nlev + 2) * rows, :]
    qh = q * jnp.exp(b)
    kkt = (k * jnp.exp(btot - b)).T
    ebt = jnp.exp(btot).T
    vh = v.astype(BF16)
    row = lax.broadcasted_iota(jnp.int32, (rows, A_DK), 0)
    col = lax.broadcasted_iota(jnp.int32, (A_DK, rows), 1)
    sh = int(np.log2(seg_len))
    for s in range(nseq):
        st = s_ref[s]
        o = o + _dot(jnp.where((row >> sh) == s, qh, 0.0).astype(BF16), st.astype(BF16))
        decay = jnp.broadcast_to(ebt[:, s * seg_len:s * seg_len + 1], (A_DK, A_DV))
        so_ref[s] = decay * st + _dot(jnp.where((col >> sh) == s, kkt, 0.0).astype(BF16), vh)
    o_ref[...] = _hgrn_finish(o, an_ref[...], ga_ref[...])


def _hgrn_decode(qa, lf, ia, ga, state, layer, an, seg_len):
    n = qa.shape[0]
    rows = HGRN_ROWS
    nseq = rows // seg_len
    w = jnp.asarray(_level_matrix(rows, seg_len, True), BF16)
    blk = pl.BlockSpec((rows, A_DK), lambda i, h: (i, h))
    return pl.pallas_call(
        functools.partial(_hgrn_decode_body, seg_len=seg_len),
        grid=(n // rows, A_HEADS),
        in_specs=[blk, blk, blk, blk,
                  pl.BlockSpec((None, nseq, None, A_DK, A_DV), lambda i, h: (layer, i, h, 0, 0)),
                  _resident(w.shape), _resident((1, A_DV))],
        out_specs=[blk, pl.BlockSpec((nseq, None, A_DK, A_DV), lambda i, h: (i, h, 0, 0))],
        out_shape=[jax.ShapeDtypeStruct((n, A_WIDTH), BF16), jax.ShapeDtypeStruct(state.shape[1:], F32)],
        compiler_params=_params("parallel", "parallel"),
        name="hgrn_decode",
    )(qa, lf, ia, ga, state, w, an)


def _sb_scores(qn, k, nbias, mask, kv_transposed):
    if kv_transposed:
        zn = _dot(qn, k) + nbias
    else:
        zn = lax.dot_general(qn, k, _NT, preferred_element_type=F32) + nbias
    neg_abs = lax.bitcast_convert_type(lax.bitcast_convert_type(zn, jnp.int32) | jnp.int32(-2 ** 31), F32)
    lr = jnp.minimum(zn, 0.0) - jnp.log(1.0 + jnp.exp2(neg_abs)) * LOG2E
    if mask is not None:
        lr = jnp.where(mask, lr, 0.0)
    return zn, lr.astype(BF16)


def _sb_weights(scores, v, run, tri, mask, kv_transposed):
    zn, lr = scores
    tail = _dot(lr, tri)
    a = jnp.exp2(tail - zn + run)
    if mask is not None:
        a = jnp.where(mask, a, 0.0)
    a = a.astype(BF16)
    if kv_transposed:
        pv = lax.dot_general(a, v, _NT, preferred_element_type=F32)
    else:
        pv = _dot(a, v)
    return pv, run + tail[:, 0:1]


def _sb_tile(qn, k, v, nbias, run, tri, mask, kv_transposed):
    return _sb_weights(_sb_scores(qn, k, nbias, mask, kv_transposed), v, run, tri, mask, kv_transposed)


def _sb_prompt_body(bias_ref, q_ref, k_ref, v_ref, tri_ref, o_ref, acc_ref, *slot_refs):
    slots = (slot_refs[:2], slot_refs[2:])
    pair = pl.program_id(1)
    i = pl.program_id(2)
    t = SB_TILE
    q = q_ref[...]
    tri = tri_ref[...]
    lane = lax.broadcasted_iota(jnp.int32, q.shape, 1)
    row = lax.broadcasted_iota(jnp.int32, (t, t), 0)
    col = lax.broadcasted_iota(jnp.int32, (t, t), 1)
    causal = col < row
    qhs = [jnp.where((lane < SB_HEAD_DIM) == (hh == 0), q, jnp.zeros_like(q)) for hh in range(2)]
    nbias = [bias_ref[2 * pair + hh] for hh in range(2)]
    heads = range(2)

    def scores(j, slot):
        for hh in heads:
            for ref, val in zip(slots[slot], _sb_scores(qhs[hh], k_ref[j], nbias[hh], None, True)):
                ref[hh] = val

    def weights(j, slot, runs):
        out = []
        for hh in heads:
            sc = tuple(ref[hh] for ref in slots[slot])
            pv, run = _sb_weights(sc, v_ref[j], runs[hh], tri, None, True)
            acc_ref[hh] += pv
            out.append(run)
        return tuple(out)

    runs = []
    for hh in heads:
        pv, run = _sb_tile(qhs[hh], k_ref[i], v_ref[i], nbias[hh], jnp.zeros((t, 1), F32), tri, causal, True)
        acc_ref[hh] = pv
        runs.append(run)

    @pl.when(i > 0)
    def _():
        scores(i - 1, 0)

        def step(n, runs):
            j = i - 1 - 2 * n
            scores(j - 1, 1)
            runs = weights(j, 0, runs)
            scores(jnp.maximum(j - 2, 0), 0)
            return weights(j - 1, 1, runs)

        runs_ = lax.fori_loop(0, i // 2, step, tuple(runs))

        @pl.when(i % 2 == 1)
        def _():
            weights(0, 0, runs_)

    o_ref[...] = jnp.where(lane < SB_HEAD_DIM, acc_ref[0], acc_ref[1]).astype(BF16)


def _lower_tri(n):
    r = np.arange(n)
    return jnp.asarray((r[:, None] >= r[None, :]).astype(np.float32), BF16)


def _sb_prompt(qb, kth, vth, bias):
    n = qb.shape[0]
    batch, nq, _, t = kth.shape
    assert t == SB_TILE
    tri = _lower_tri(t)
    qblk = pl.BlockSpec((t, LANES), lambda b, p, i: (b * nq + i, p))
    kvblk = pl.BlockSpec((None, nq, LANES, t), lambda b, p, i: (b, 0, p, 0))
    return pl.pallas_call(
        _sb_prompt_body,
        grid=(batch, SB_WIDTH // LANES, nq),
        in_specs=[pl.BlockSpec(memory_space=pltpu.SMEM), qblk, kvblk, kvblk, _resident((t, t))],
        out_specs=qblk,
        out_shape=jax.ShapeDtypeStruct((n, SB_WIDTH), BF16),
        scratch_shapes=[pltpu.VMEM((2, t, LANES), F32)]
                       + [pltpu.VMEM((2, t, t), F32), pltpu.VMEM((2, t, t), BF16)] * 2,
        compiler_params=_params("parallel", "parallel", "arbitrary"),
        name="sb_prompt",
    )(bias, qb, kth, vth, tri)


def _sb_decode_body(pt_ref, q_ref, kn_ref, vn_ref, bias_ref, tri_ref, *refs, n_pages, n_new):
    kp = refs[:n_pages]
    vp = refs[n_pages:2 * n_pages]
    o_ref = refs[2 * n_pages]
    kpad, vpad = refs[2 * n_pages + 1:]
    del pt_ref
    rows = n_new * SB_HEADS
    q = q_ref[...]
    qbd = jnp.concatenate([jnp.broadcast_to(q[t:t + 1, :], (SB_HEADS, SB_WIDTH)) for t in range(n_new)], axis=0)
    row = lax.broadcasted_iota(jnp.int32, (rows, SB_WIDTH), 0)
    lane = lax.broadcasted_iota(jnp.int32, (rows, SB_WIDTH), 1)
    own = (lane // SB_HEAD_DIM) == (row % SB_HEADS)
    qbd = jnp.where(own, qbd, 0.0).astype(BF16)
    bias = bias_ref[...]
    tri = tri_ref[...]

    kpad[...] = jnp.zeros_like(kpad)
    vpad[...] = jnp.zeros_like(vpad)
    kpad[0:n_new, :] = kn_ref[...]
    vpad[0:n_new, :] = vn_ref[...]
    mrow = lax.broadcasted_iota(jnp.int32, (rows, PAGE_SIZE), 0)
    mcol = lax.broadcasted_iota(jnp.int32, (rows, PAGE_SIZE), 1)
    new_mask = mcol < (mrow // SB_HEADS)
    page_t = lambda ref: ref[...].reshape(SB_WIDTH, PAGE_SIZE).astype(BF16)
    scores = [_sb_scores(qbd, kpad[...].astype(BF16), bias, new_mask, False)]
    scores += [_sb_scores(qbd, page_t(kp[p]), bias, None, True) for p in range(n_pages - 1, -1, -1)]
    tail = _dot(jnp.concatenate([lr for _, lr in scores], axis=0), tri)
    run = jnp.zeros((rows, 1), F32)
    acc = None
    for s, (zn, _) in enumerate(scores):
        tl = tail[s * rows:(s + 1) * rows]
        a = jnp.exp2(tl - zn + run)
        run = run + tl[:, 0:1]
        if s == 0:
            acc = _dot(jnp.where(new_mask, a, 0.0).astype(BF16), vpad[...].astype(BF16))
        else:
            acc = acc + lax.dot_general(a.astype(BF16), page_t(vp[n_pages - s]), _NT, preferred_element_type=F32)
    acc = jnp.where(own, acc, 0.0)
    o_ref[...] = jnp.sum(acc.reshape(n_new, SB_HEADS, SB_WIDTH), axis=1)


def _sb_decode(qb, kb, vb, cache_kt, cache_vt, layer, page_table, bias_rows):
    n_seq, n_new, _ = qb.shape
    n_pages = page_table.shape[1]
    rows = n_new * SB_HEADS
    tri = _lower_tri(PAGE_SIZE)
    tok = pl.BlockSpec((None, n_new, SB_WIDTH), lambda s, pt: (s, 0, 0))

    def page(p):
        return pl.BlockSpec((None, None, SB_HEADS, SB_HEAD_DIM, PAGE_SIZE), lambda s, pt: (layer, pt[s, p], 0, 0, 0))

    grid_spec = pltpu.PrefetchScalarGridSpec(
        num_scalar_prefetch=1,
        grid=(n_seq,),
        in_specs=[tok, tok, tok, _resident((rows, PAGE_SIZE)), _resident((PAGE_SIZE, PAGE_SIZE))]
                 + [page(p) for p in range(n_pages)] * 2,
        out_specs=tok,
        scratch_shapes=[pltpu.VMEM((PAGE_SIZE, SB_WIDTH), F32), pltpu.VMEM((PAGE_SIZE, SB_WIDTH), F32)],
    )
    return pl.pallas_call(
        functools.partial(_sb_decode_body, n_pages=n_pages, n_new=n_new),
        grid_spec=grid_spec,
        out_shape=jax.ShapeDtypeStruct((n_seq, n_new, SB_WIDTH), F32),
        compiler_params=_params("arbitrary"),
        name="sb_decode",
    )(page_table, qb, kb, vb, bias_rows, tri, *([cache_kt] * n_pages), *([cache_vt] * n_pages))


def _conv_finish(y, cb, lg, lb):
    y = y + cb
    mu = jnp.mean(y, axis=-1, keepdims=True)
    yc = y - mu
    var = jnp.mean(yc * yc, axis=-1, keepdims=True)
    y = yc * lax.rsqrt(var + EPS) * lg + lb
    return y * _sigmoid(y)


def _conv_prompt_body(u_ref, prev_ref, w_ref, cb_ref, lg_ref, lb_ref, o_ref, buf, *, tile, chunk):
    i = pl.program_id(1)
    prev = prev_ref[...]
    buf[0, 0:CONV_HALO, :] = jnp.where(i > 0, prev, jnp.zeros_like(prev))
    buf[0, CONV_HALO:CONV_HALO + tile, :] = u_ref[...]
    span = CONV_HALO + tile - SUBLANES
    for s in range(1, SUBLANES):
        buf[s, 0:span, :] = buf[0, s:s + span, :]
    base = CONV_HALO - (CONV_W - 1)
    for r0 in range(0, tile, chunk):
        acc = jnp.zeros((chunk, C_CH), F32)
        for t in range(CONV_W):
            off = r0 + base + t
            lo = off - off % SUBLANES
            acc = acc + w_ref[t:t + 1, :] * buf[off % SUBLANES, lo:lo + chunk, :]
        o_ref[r0:r0 + chunk, :] = _conv_finish(acc, cb_ref[...], lg_ref[...], lb_ref[...]).astype(BF16)


def _conv_prompt(u, w, cb, lg, lb, batch, tile=256, chunk=32):
    n = u.shape[0]
    nt = n // batch // tile
    per = tile // CONV_HALO
    cur = pl.BlockSpec((tile, C_CH), lambda b, i: (b * nt + i, 0))
    prev = pl.BlockSpec((CONV_HALO, C_CH), lambda b, i: (jnp.maximum((b * nt + i) * per - 1, 0), 0))
    vec = _resident((1, C_CH))
    return pl.pallas_call(
        functools.partial(_conv_prompt_body, tile=tile, chunk=chunk),
        grid=(batch, nt),
        in_specs=[cur, prev, _resident((CONV_W, C_CH)), vec, vec, vec],
        out_specs=cur,
        out_shape=jax.ShapeDtypeStruct((n, C_CH), BF16),
        scratch_shapes=[pltpu.VMEM((SUBLANES, CONV_HALO + tile, C_CH), F32)],
        compiler_params=_params("parallel", "parallel"),
        name="conv_prompt",
    )(u, u, w, cb, lg, lb)


def _conv_decode_body(u_ref, st_ref, w_ref, cb_ref, lg_ref, lb_ref, o_ref, so_ref, *, n_new):
    ctx = CONV_W - 1
    full = lambda j: st_ref[j] if j < ctx else u_ref[j - ctx]
    for t in range(n_new):
        acc = w_ref[0:1, :] * full(t)
        for j in range(1, CONV_W):
            acc = acc + w_ref[j:j + 1, :] * full(t + j)
        o_ref[t] = _conv_finish(acc, cb_ref[...], lg_ref[...], lb_ref[...])
    for j in range(ctx):
        so_ref[j] = full(j + n_new)


def _conv_decode(u, state, layer, w, cb, lg, lb, per_step=16):
    n_new, n_seq, _ = u.shape
    ctx = CONV_W - 1
    ublk = pl.BlockSpec((n_new, per_step, C_CH), lambda i: (0, i, 0))
    vec = _resident((1, C_CH))
    return pl.pallas_call(
        functools.partial(_conv_decode_body, n_new=n_new),
        grid=(n_seq // per_step,),
        in_specs=[ublk, pl.BlockSpec((None, ctx, per_step, C_CH), lambda i: (layer, 0, i, 0)),
                  _resident((CONV_W, C_CH)), vec, vec, vec],
        out_specs=[ublk, pl.BlockSpec((ctx, per_step, C_CH), lambda i: (0, i, 0))],
        out_shape=[jax.ShapeDtypeStruct((n_new, n_seq, C_CH), F32), jax.ShapeDtypeStruct((ctx, n_seq, C_CH), F32)],
        compiler_params=_params("parallel"),
        name="conv_decode",
    )(u, state, w, cb, lg, lb)


def _mix_out_body(x_ref, oa_ref, ob_ref, oc_ref, gt_ref, wpa_ref, wpb_ref, wpc_ref, wo_ref, g_ref, y_ref):
    m = (gt_ref[:, 0:D_MODEL] * _dot(oa_ref[...], wpa_ref[...])
         + gt_ref[:, D_MODEL:2 * D_MODEL] * _dot(ob_ref[...], wpb_ref[...])
         + gt_ref[:, 2 * D_MODEL:3 * D_MODEL] * _dot(oc_ref[...], wpc_ref[...]))
    out = _dot(m.astype(BF16), wo_ref[...])
    y_ref[...] = x_ref[...] + _rms(out, g_ref[...])


def _mix_out(x, oa, ob, oc, gt, wpa, wpb, wpc, wo, g, tm):
    n = x.shape[0]
    row = lambda width: pl.BlockSpec((tm, width), lambda i: (i, 0))
    return pl.pallas_call(
        _mix_out_body,
        grid=(n // tm,),
        in_specs=[row(D_MODEL), row(A_WIDTH), row(SB_WIDTH), row(C_CH), row(N_BRANCH * D_MODEL),
                  _resident(wpa.shape), _resident(wpb.shape), _resident(wpc.shape), _resident(wo.shape),
                  _resident((1, D_MODEL))],
        out_specs=row(D_MODEL),
        out_shape=jax.ShapeDtypeStruct((n, D_MODEL), F32),
        compiler_params=_params("parallel"),
        name="mix_out",
    )(x, oa, ob, oc, gt, wpa, wpb, wpc, wo, g)


def _ffn_body(x_ref, g1_ref, w1_ref, w2_ref, g2_ref, y_ref, *, chunk):
    x = x_ref[...]
    h = _rms(x, g1_ref[...]).astype(BF16)
    acc = None
    for c0 in range(0, D_FF, chunk):
        a = jnp.maximum(_dot(h, w1_ref[:, c0:c0 + chunk]), 0.0)
        part = _dot((a * a).astype(BF16), w2_ref[c0:c0 + chunk, :])
        acc = part if acc is None else acc + part
    y_ref[...] = x + _rms(acc, g2_ref[...])


def _ffn(x, g1, w1, w2, g2, tm, chunk=1024):
    n = x.shape[0]
    row = pl.BlockSpec((tm, D_MODEL), lambda i: (i, 0))
    vec = _resident((1, D_MODEL))
    return pl.pallas_call(
        functools.partial(_ffn_body, chunk=chunk),
        grid=(n // tm,),
        in_specs=[row, vec, _resident(w1.shape), _resident(w2.shape), vec],
        out_specs=row,
        out_shape=jax.ShapeDtypeStruct((n, D_MODEL), F32),
        compiler_params=_params("parallel"),
        name="ffn",
    )(x, g1, w1, w2, g2)


def _trunk(x3, wts, lb, cache):
    batch, seq, _ = x3.shape
    n = batch * seq
    x = x3.reshape(n, D_MODEL)
    tm = SB_TILE
    depth = wts["w_in"].shape[0]
    ks, vs, ss, cs = [], [], [], []
    for l in range(depth):
        vec = lambda name: wts[name][l].reshape(1, -1)
        conv_args = (wts["conv_w"][l], vec("conv_b"), vec("ln_g"), vec("ln_b"))
        proj_args = (x, vec("g_pre_mix"), wts["w_in"][l], lb[l].reshape(1, -1), l == 0, tm)
        nbias = -LOG2E * wts["sb_bias"][l]
        if cache is None:
            qa, lf, ia, ga, qb, u, gt, kt, vt, kth, vth = _in_proj(*proj_args, batch=batch)
            oa, s_new = _hgrn_prompt(qa, lf, ia, ga, vec("a_norm"), batch)
            ob = _sb_prompt(qb, kth, vth, nbias)
            oc = _conv_prompt(u, *conv_args, batch)
            c_new = u.reshape(batch, seq, C_CH)[:, seq - (CONV_W - 1):]
            k_new, v_new = (a.reshape(batch, SB_HEADS, SB_HEAD_DIM, seq).transpose(0, 3, 1, 2) for a in (kt, vt))
        else:
            cache_kt, cache_vt, page_table, state_hgrn, state_conv_t = cache
            qa, lf, ia, ga, qb, u, gt, kb, vb = _in_proj(*proj_args)
            oa, s_new = _hgrn_decode(qa, lf, ia, ga, state_hgrn, l, vec("a_norm"), seq)
            bias_rows = jnp.broadcast_to(jnp.tile(nbias, seq)[:, None], (seq * SB_HEADS, PAGE_SIZE))
            tok = lambda a: a.reshape(batch, seq, -1)
            ob = _sb_decode(tok(qb).astype(F32), tok(kb), tok(vb), cache_kt, cache_vt, l, page_table, bias_rows)
            ob = ob.reshape(n, SB_WIDTH).astype(BF16)
            oc, c_new = _conv_decode(tok(u).transpose(1, 0, 2), state_conv_t, l, *conv_args)
            oc = oc.transpose(1, 0, 2).reshape(n, C_CH).astype(BF16)
            c_new = c_new.transpose(1, 0, 2)
            k_new, v_new = (a.reshape(batch, seq, SB_HEADS, SB_HEAD_DIM) for a in (kb, vb))
        x = _mix_out(x, oa, ob, oc, gt, wts["w_pa"][l], wts["w_pb"][l], wts["w_pc"][l], wts["w_o"][l],
                     vec("g_post_mix"), tm)
        x = _ffn(x, vec("g_pre_ffn"), wts["w_ff1"][l], wts["w_ff2"][l], vec("g_post_ffn"), tm)
        ks.append(k_new)
        vs.append(v_new)
        ss.append(s_new)
        cs.append(c_new)
    return x.reshape(batch, seq, D_MODEL), jnp.stack(ks), jnp.stack(vs), jnp.stack(ss), jnp.stack(cs)


def kernel(x_prompt, x_sample, cache_k, cache_v, page_table, state_hgrn, state_conv, g_pre_mix, g_post_mix,
           g_pre_ffn, g_post_ffn, w_in, hgrn_lb, a_norm, w_pa, w_pb, sb_bias, conv_w, conv_b, ln_g, ln_b, w_pc,
           w_o, w_ff1, w_ff2):
    wts = dict(g_pre_mix=g_pre_mix, g_post_mix=g_post_mix, g_pre_ffn=g_pre_ffn, g_post_ffn=g_post_ffn,
               a_norm=a_norm, sb_bias=sb_bias, conv_w=conv_w, conv_b=conv_b, ln_g=ln_g, ln_b=ln_b,
               w_in=w_in.astype(BF16), w_pa=w_pa.astype(BF16), w_pb=w_pb.astype(BF16), w_pc=w_pc.astype(BF16),
               w_o=w_o.astype(BF16), w_ff1=w_ff1.astype(BF16), w_ff2=w_ff2.astype(BF16))
    lb = jnp.cumsum(jax.nn.softmax(hgrn_lb.astype(F32), axis=0), axis=0)
    lb = lb - lb[0:1]
    y_p, k_p, v_p, s_p, c_p = _trunk(x_prompt, wts, lb, None)
    cache = (cache_k.transpose(0, 1, 3, 4, 2), cache_v.transpose(0, 1, 3, 4, 2), page_table, state_hgrn,
             state_conv.transpose(0, 2, 1, 3))
    y_s, k_s, v_s, s_s, c_s = _trunk(x_sample, wts, lb, cache)
    return (y_p, y_s, k_p, v_p, s_p, c_p, k_s, v_s, s_s, c_s)
```

```python
import functools

import numpy as np
import jax
import jax.numpy as jnp
from jax import lax
from jax.experimental import pallas as pl
from jax.experimental.pallas import tpu as pltpu

F32 = jnp.float32
BF16 = jnp.bfloat16

D_MODEL = 1024
A_HEADS = 4
A_DK = 128
A_DV = 128
A_WIDTH = A_HEADS * A_DK
SB_HEADS = 8
SB_HEAD_DIM = 64
SB_WIDTH = SB_HEADS * SB_HEAD_DIM
C_CH = 512
CONV_W = 31
D_FF = 4 * D_MODEL
N_BRANCH = 3
EPS = 1e-6
PAGE_SIZE = 128
KV_OFFSET = 4 * A_WIDTH + SB_WIDTH
LOG2E = 1.4426950408889634
SB_QSCALE = -LOG2E * SB_HEAD_DIM ** -0.5

LANES = 128
SUBLANES = 8
VMEM_LIMIT_BYTES = 56 * 1024 * 1024

HGRN_ROWS = 128
SB_TILE = 256
CONV_HALO = 32

_NT = (((1,), (1,)), ((), ()))


def _params(*semantics):
    return pltpu.CompilerParams(dimension_semantics=semantics, vmem_limit_bytes=VMEM_LIMIT_BYTES)


def _resident(shape):
    nd = len(shape)
    return pl.BlockSpec(shape, lambda *_: (0,) * nd, pipeline_mode=pl.Buffered(1))


def _sigmoid(x):
    return 1.0 / (1.0 + jnp.exp(-x))


def _softplus(x):
    return jnp.maximum(x, 0.0) + jnp.log1p(jnp.exp(-jnp.abs(x)))


def _rms(x, g):
    return x * lax.rsqrt(jnp.mean(x * x, axis=-1, keepdims=True) + EPS) * g


def _dot(a, b):
    return jnp.dot(a, b, preferred_element_type=F32)


def _in_proj_body(x_ref, g_ref, w_ref, lb_ref, *refs, layer0, transposed_kv, n_alias):
    if transposed_kv:
        wkv_ref = refs[0]
        qa_ref, lf_ref, ia_ref, ga_ref, qb_ref, u_ref, gt_ref, kt_ref, vt_ref, kth_ref, vth_ref = refs[1 + n_alias:]
    else:
        qa_ref, lf_ref, ia_ref, ga_ref, qb_ref, u_ref, gt_ref, kb_ref, vb_ref = refs
    h = _rms(x_ref[...], g_ref[...]).astype(BF16)

    def seg(lo, n):
        return _dot(h, w_ref[:, lo:lo + n])

    qa = seg(0, A_WIDTH)
    qa_ref[...] = qa * _sigmoid(qa)
    fa = seg(A_WIDTH, A_WIDTH)
    if layer0:
        lf_ref[...] = -_softplus(-fa)
    else:
        lb = lb_ref[...]
        lf_ref[...] = jnp.log(lb + (1.0 - lb) * _sigmoid(fa))
    ia_ref[...] = seg(2 * A_WIDTH, A_WIDTH)
    ga = seg(3 * A_WIDTH, A_WIDTH)
    ga_ref[...] = ga * _sigmoid(ga)
    qb_ref[...] = (seg(4 * A_WIDTH, SB_WIDTH) * SB_QSCALE).astype(BF16)
    if transposed_kv:
        kv = lax.dot_general(wkv_ref[...], h, _NT, preferred_element_type=F32)
        kt_ref[...] = kv[0:SB_WIDTH]
        vt_ref[...] = kv[SB_WIDTH:2 * SB_WIDTH]
        kth_ref[...] = kv[0:SB_WIDTH].astype(BF16)
        vth_ref[...] = kv[SB_WIDTH:2 * SB_WIDTH].astype(BF16)
    else:
        kb_ref[...] = seg(KV_OFFSET, SB_WIDTH)
        vb_ref[...] = seg(KV_OFFSET + SB_WIDTH, SB_WIDTH)
    o = KV_OFFSET + 2 * SB_WIDTH
    u_ref[...] = seg(o, C_CH) * _sigmoid(seg(o + C_CH, C_CH))
    o += 2 * C_CH
    for j in range(N_BRANCH):
        gt_ref[:, j * D_MODEL:(j + 1) * D_MODEL] = _sigmoid(seg(o + j * D_MODEL, D_MODEL)).astype(BF16)


def _in_proj(x, g, w, lb, layer, tm, batch=None, depth=None, kv_stack=None):
    n = x.shape[0]
    in_width = w.shape[1]
    row = lambda width: pl.BlockSpec((tm, width), lambda i: (i, 0))
    widths = [A_WIDTH] * 4 + [SB_WIDTH, C_CH, N_BRANCH * D_MODEL]
    dtypes = [F32] * 4 + [BF16, F32, BF16]
    in_specs = [row(D_MODEL), _resident((1, D_MODEL)), _resident((D_MODEL, in_width)), _resident((1, A_WIDTH))]
    out_specs = [row(wd) for wd in widths]
    out_shape = [jax.ShapeDtypeStruct((n, wd), dt) for wd, dt in zip(widths, dtypes)]
    args = [x, g, w, lb]
    aliases = {}
    if batch is None:
        out_specs += [row(SB_WIDTH)] * 2
        out_shape += [jax.ShapeDtypeStruct((n, SB_WIDTH), F32)] * 2
    else:
        nt = n // batch // tm
        in_specs.append(_resident((2 * SB_WIDTH, D_MODEL)))
        args.append(w[:, KV_OFFSET:KV_OFFSET + 2 * SB_WIDTH].T)
        if kv_stack is not None:
            aliases = {len(args): len(out_specs), len(args) + 1: len(out_specs) + 1}
            in_specs += [pl.BlockSpec(memory_space=pl.ANY)] * 2
            args += list(kv_stack)
        out_specs += [pl.BlockSpec((None, None, SB_WIDTH, tm), lambda i: (layer, i // nt, 0, i % nt))] * 2
        out_specs += [pl.BlockSpec((None, None, SB_WIDTH, tm), lambda i: (i // nt, i % nt, 0, 0))] * 2
        out_shape += [jax.ShapeDtypeStruct((depth, batch, SB_WIDTH, nt * tm), F32)] * 2
        out_shape += [jax.ShapeDtypeStruct((batch, nt, SB_WIDTH, tm), BF16)] * 2
    return pl.pallas_call(
        functools.partial(_in_proj_body, layer0=layer == 0, transposed_kv=batch is not None, n_alias=len(aliases)),
        grid=(n // tm,),
        in_specs=in_specs,
        out_specs=out_specs,
        out_shape=out_shape,
        input_output_aliases=aliases,
        compiler_params=_params("parallel"),
        name="in_proj",
    )(*args)


def _level_matrix(rows, seg_len, with_total):
    r = np.arange(rows)[:, None]
    u = np.arange(rows)[None, :]
    mats = []
    m = seg_len // 2
    while m >= 1:
        start = (r // (2 * m)) * (2 * m)
        bound = start + m - 1
        is_q = (r - start) >= m
        wq = is_q & (u > bound) & (u <= r)
        wk = (~is_q) & (u > r) & (u <= bound)
        mats.append(wq | wk)
        m //= 2
    seg = (r // seg_len) == (u // seg_len)
    mats.append(seg & (u <= r))
    if with_total:
        mats.append(seg)
    return np.concatenate(mats, axis=0).astype(np.float32)


def _num_levels(seg_len):
    return int(np.log2(seg_len))


def _level_exponents(w, g):
    hi = g.astype(BF16)
    lo = (g - hi.astype(F32)).astype(BF16)
    return _dot(w, hi) + _dot(w, lo)


def _hgrn_intra(q, k, v, e_of, seg_len):
    rows = q.shape[0]
    row = lax.broadcasted_iota(jnp.int32, (rows, rows), 0)
    col = lax.broadcasted_iota(jnp.int32, (rows, rows), 1)
    rowq = lax.broadcasted_iota(jnp.int32, q.shape, 0)
    att = None
    m = seg_len // 2
    lev = 0
    while m >= 1:
        ee = jnp.exp(e_of(lev))
        is_q = (rowq & (2 * m - 1)) >= m
        qt = jnp.where(is_q, q * ee, 0.0).astype(BF16)
        kt = jnp.where(is_q, 0.0, k * ee).astype(BF16)
        a = lax.dot_general(qt, kt, _NT, preferred_element_type=F32)
        if 2 * m < rows:
            sh = int(np.log2(2 * m))
            a = jnp.where((row >> sh) == (col >> sh), a, 0.0)
        att = a if att is None else att + a
        m //= 2
        lev += 1
    d = jnp.sum(q * k, axis=1, keepdims=True)
    return _dot(att.astype(BF16), v.astype(BF16)) + d * v


def _hgrn_finish(o, an, ga):
    return (_rms(o, an) * ga).astype(BF16)


def _hgrn_prompt_body(q_ref, lf_ref, v_ref, ga_ref, w_ref, an_ref, o_ref, s_ref, st_scr, e_scr):
    c = pl.program_id(1)
    rows = HGRN_ROWS
    nlev = _num_levels(rows)

    @pl.when(c == 0)
    def _():
        st_scr[...] = jnp.zeros_like(st_scr)

    g = lf_ref[...]
    e_scr[...] = _level_exponents(w_ref[...], g)

    for h in range(A_HEADS):
        sl = slice(h * A_DK, (h + 1) * A_DK)
        q = q_ref[:, sl]
        v = v_ref[:, sl]
        k = 1.0 - jnp.exp(g[:, sl])
        o = _hgrn_intra(q, k, v, lambda lev: e_scr[lev * rows:(lev + 1) * rows, sl], rows)
        b = e_scr[nlev * rows:(nlev + 1) * rows, sl]
        st = st_scr[h]
        o = o + lax.dot_general((q * jnp.exp(b)).astype(BF16), st.astype(BF16), _NT, preferred_element_type=F32)
        bl = b[rows - 1:rows, :]
        kk = (k * jnp.exp(bl - b)).astype(BF16)
        st_scr[h] = st * jnp.exp(bl) + _dot(v.T.astype(BF16), kk)
        o_ref[:, sl] = _hgrn_finish(o, an_ref[...], ga_ref[:, sl])

    @pl.when(c == pl.num_programs(1) - 1)
    def _():
        for h in range(A_HEADS):
            s_ref[h] = st_scr[h].T


def _hgrn_prompt(qa, lf, ia, ga, an, batch):
    n = qa.shape[0]
    rows = HGRN_ROWS
    nc = n // batch // rows
    w = jnp.asarray(_level_matrix(rows, rows, False), BF16)
    blk = pl.BlockSpec((rows, A_WIDTH), lambda b, c: (b * nc + c, 0))
    return pl.pallas_call(
        _hgrn_prompt_body,
        grid=(batch, nc),
        in_specs=[blk, blk, blk, blk, _resident(w.shape), _resident((1, A_DV))],
        out_specs=[blk, pl.BlockSpec((None, A_HEADS, A_DK, A_DV), lambda b, c: (b, 0, 0, 0))],
        out_shape=[jax.ShapeDtypeStruct((n, A_WIDTH), BF16),
                   jax.ShapeDtypeStruct((batch, A_HEADS, A_DK, A_DV), F32)],
        scratch_shapes=[pltpu.VMEM((A_HEADS, A_DV, A_DK), F32), pltpu.VMEM(w.shape[:1] + (A_WIDTH,), F32)],
        compiler_params=_params("parallel", "arbitrary"),
        name="hgrn_prompt",
    )(qa, lf, ia, ga, w, an)


def _hgrn_decode_body(q_ref, lf_ref, v_ref, ga_ref, s_ref, w_ref, an_ref, *refs, seg_len, n_alias):
    o_ref, so_ref = refs[n_alias:]
    rows = q_ref.shape[0]
    nseq = rows // seg_len
    nlev = _num_levels(seg_len)
    g = lf_ref[...]
    e = _level_exponents(w_ref[...], g)
    q = q_ref[...]
    v = v_ref[...]
    k = 1.0 - jnp.exp(g)
    o = _hgrn_intra(q, k, v, lambda lev: e[lev * rows:(lev + 1) * rows, :], seg_len)
    b = e[nlev * rows:(nlev + 1) * rows, :]
    btot = e[(nlev + 1) * rows:(nlev + 2) * rows, :]
    qh = q * jnp.exp(b)
    kkt = (k * jnp.exp(btot - b)).T
    ebt = jnp.exp(btot).T
    vh = v.astype(BF16)
    row = lax.broadcasted_iota(jnp.int32, (rows, A_DK), 0)
    col = lax.broadcasted_iota(jnp.int32, (A_DK, rows), 1)
    sh = int(np.log2(seg_len))
    for s in range(nseq):
        st = s_ref[s]
        o = o + _dot(jnp.where((row >> sh) == s, qh, 0.0).astype(BF16), st.astype(BF16))
        decay = jnp.broadcast_to(ebt[:, s * seg_len:s * seg_len + 1], (A_DK, A_DV))
        so_ref[s] = decay * st + _dot(jnp.where((col >> sh) == s, kkt, 0.0).astype(BF16), vh)
    o_ref[...] = _hgrn_finish(o, an_ref[...], ga_ref[...])


def _hgrn_decode(qa, lf, ia, ga, state, layer, an, seg_len, new_state=None):
    n = qa.shape[0]
    rows = HGRN_ROWS
    nseq = rows // seg_len
    w = jnp.asarray(_level_matrix(rows, seg_len, True), BF16)
    blk = pl.BlockSpec((rows, A_DK), lambda i, h: (i, h))
    sblk = pl.BlockSpec((None, nseq, None, A_DK, A_DV), lambda i, h: (layer, i, h, 0, 0))
    in_specs = [blk, blk, blk, blk, sblk, _resident(w.shape), _resident((1, A_DV))]
    args = [qa, lf, ia, ga, state, w, an]
    aliases = {}
    if new_state is not None:
        aliases = {len(args): 1}
        in_specs.append(pl.BlockSpec(memory_space=pl.ANY))
        args.append(new_state)
    return pl.pallas_call(
        functools.partial(_hgrn_decode_body, seg_len=seg_len, n_alias=len(aliases)),
        grid=(n // rows, A_HEADS),
        in_specs=in_specs,
        out_specs=[blk, sblk],
        out_shape=[jax.ShapeDtypeStruct((n, A_WIDTH), BF16), jax.ShapeDtypeStruct(state.shape, F32)],
        input_output_aliases=aliases,
        compiler_params=_params("parallel", "parallel"),
        name="hgrn_decode",
    )(*args)


def _sb_scores(qn, k, nbias, mask, kv_transposed):
    if kv_transposed:
        zn = _dot(qn, k) + nbias
    else:
        zn = lax.dot_general(qn, k, _NT, preferred_element_type=F32) + nbias
    lr = jnp.minimum(zn, 0.0) - jnp.log(1.0 + jnp.exp2(-jnp.abs(zn))) * LOG2E
    if mask is not None:
        lr = jnp.where(mask, lr, 0.0)
    return zn, lr.astype(BF16)


def _sb_weights(scores, v, run, tri, mask, kv_transposed):
    zn, lr = scores
    tail = _dot(lr, tri)
    a = jnp.exp2(tail - zn + run)
    if mask is not None:
        a = jnp.where(mask, a, 0.0)
    a = a.astype(BF16)
    if kv_transposed:
        pv = lax.dot_general(a, v, _NT, preferred_element_type=F32)
    else:
        pv = _dot(a, v)
    return pv, run + tail[:, 0:1]


def _sb_tile(qn, k, v, nbias, run, tri, mask, kv_transposed):
    return _sb_weights(_sb_scores(qn, k, nbias, mask, kv_transposed), v, run, tri, mask, kv_transposed)


def _sb_prompt_body(bias_ref, q_ref, k_ref, v_ref, tri_ref, o_ref, acc_ref, *slot_refs):
    slots = (slot_refs[:2], slot_refs[2:])
    pair = pl.program_id(0)
    i = pl.program_id(1)
    t = SB_TILE
    batch = q_ref.shape[0]
    tri = tri_ref[...]
    lane = lax.broadcasted_iota(jnp.int32, (t, LANES), 1)
    row = lax.broadcasted_iota(jnp.int32, (t, t), 0)
    col = lax.broadcasted_iota(jnp.int32, (t, t), 1)
    causal = col < row
    chains = [(b, hh) for b in range(batch) for hh in range(2)]
    qn = [jnp.where((lane < SB_HEAD_DIM) == (hh == 0), q_ref[b], jnp.zeros((t, LANES), BF16)) for b, hh in chains]
    nbias = [bias_ref[2 * pair + hh] for _, hh in chains]

    def scores(j, slot, mask=None):
        for c, (b, _) in enumerate(chains):
            for ref, val in zip(slots[slot], _sb_scores(qn[c], k_ref[b, j], nbias[c], mask, True)):
                ref[c] = val

    def weights(j, slot, runs, mask=None):
        out = []
        for c, (b, _) in enumerate(chains):
            sc = tuple(ref[c] for ref in slots[slot])
            pv, run = _sb_weights(sc, v_ref[b, j], runs[c], tri, mask, True)
            if mask is None:
                acc_ref[c] += pv
            else:
                acc_ref[c] = pv
            out.append(run)
        return tuple(out)

    scores(i, 1, causal)
    scores(jnp.maximum(i - 1, 0), 0)
    runs = weights(i, 1, (jnp.zeros((t, 1), F32),) * len(chains), causal)

    @pl.when(i > 0)
    def _():
        def step(n, runs):
            j = i - 1 - 2 * n
            scores(j - 1, 1)
            runs = weights(j, 0, runs)
            scores(jnp.maximum(j - 2, 0), 0)
            return weights(j - 1, 1, runs)

        runs_ = lax.fori_loop(0, i // 2, step, runs)

        @pl.when(i % 2 == 1)
        def _():
            weights(0, 0, runs_)

    for b in range(batch):
        o_ref[b] = jnp.where(lane < SB_HEAD_DIM, acc_ref[2 * b], acc_ref[2 * b + 1]).astype(BF16)


def _lower_tri(n):
    r = np.arange(n)
    return jnp.asarray((r[:, None] >= r[None, :]).astype(np.float32), BF16)


def _sb_prompt(qb, kth, vth, bias):
    batch, nq, _, t = kth.shape
    assert t == SB_TILE
    nc = 2 * batch
    tri = _lower_tri(t)
    qblk = pl.BlockSpec((batch, t, LANES), lambda p, i: (0, i, p))
    kvblk = pl.BlockSpec((batch, nq, LANES, t), lambda p, i: (0, 0, p, 0))
    return pl.pallas_call(
        _sb_prompt_body,
        grid=(SB_WIDTH // LANES, nq),
        in_specs=[pl.BlockSpec(memory_space=pltpu.SMEM), qblk, kvblk, kvblk, _resident((t, t))],
        out_specs=qblk,
        out_shape=jax.ShapeDtypeStruct(qb.shape, BF16),
        scratch_shapes=[pltpu.VMEM((nc, t, LANES), F32)]
                       + [pltpu.VMEM((nc, t, t), F32), pltpu.VMEM((nc, t, t), BF16)] * 2,
        compiler_params=_params("parallel", "arbitrary"),
        name="sb_prompt",
    )(bias, qb, kth, vth, tri)


def _sb_decode_body(pt_ref, q_ref, kn_ref, vn_ref, bias_ref, tri_ref, *refs, n_pages, n_new):
    kp = refs[:n_pages]
    vp = refs[n_pages:2 * n_pages]
    o_ref = refs[2 * n_pages]
    kpad, vpad = refs[2 * n_pages + 1:]
    del pt_ref
    rows = n_new * SB_HEADS
    q = q_ref[...]
    qbd = jnp.concatenate([jnp.broadcast_to(q[t:t + 1, :], (SB_HEADS, SB_WIDTH)) for t in range(n_new)], axis=0)
    row = lax.broadcasted_iota(jnp.int32, (rows, SB_WIDTH), 0)
    lane = lax.broadcasted_iota(jnp.int32, (rows, SB_WIDTH), 1)
    own = (lane // SB_HEAD_DIM) == (row % SB_HEADS)
    qbd = jnp.where(own, qbd, 0.0).astype(BF16)
    bias = bias_ref[...]
    tri = tri_ref[...]

    kpad[...] = jnp.zeros_like(kpad)
    vpad[...] = jnp.zeros_like(vpad)
    kpad[0:n_new, :] = kn_ref[...]
    vpad[0:n_new, :] = vn_ref[...]
    mrow = lax.broadcasted_iota(jnp.int32, (rows, PAGE_SIZE), 0)
    mcol = lax.broadcasted_iota(jnp.int32, (rows, PAGE_SIZE), 1)
    new_mask = mcol < (mrow // SB_HEADS)
    page_t = lambda ref: ref[...].reshape(SB_WIDTH, PAGE_SIZE).astype(BF16)
    scores = [_sb_scores(qbd, kpad[...].astype(BF16), bias, new_mask, False)]
    scores += [_sb_scores(qbd, page_t(kp[p]), bias, None, True) for p in range(n_pages - 1, -1, -1)]
    tail = _dot(jnp.concatenate([lr for _, lr in scores], axis=0), tri)
    run = jnp.zeros((rows, 1), F32)
    acc = None
    for s, (zn, _) in enumerate(scores):
        tl = tail[s * rows:(s + 1) * rows]
        a = jnp.exp2(tl - zn + run)
        run = run + tl[:, 0:1]
        if s == 0:
            acc = _dot(jnp.where(new_mask, a, 0.0).astype(BF16), vpad[...].astype(BF16))
        else:
            acc = acc + lax.dot_general(a.astype(BF16), page_t(vp[n_pages - s]), _NT, preferred_element_type=F32)
    acc = jnp.where(own, acc, 0.0)
    o_ref[...] = jnp.sum(acc.reshape(n_new, SB_HEADS, SB_WIDTH), axis=1)


def _sb_decode(qb, kb, vb, cache_kt, cache_vt, layer, page_table, bias_rows):
    n_seq, n_new, _ = qb.shape
    n_pages = page_table.shape[1]
    rows = n_new * SB_HEADS
    tri = _lower_tri(PAGE_SIZE)
    tok = pl.BlockSpec((None, n_new, SB_WIDTH), lambda s, pt: (s, 0, 0))

    def page(p):
        return pl.BlockSpec((None, None, SB_HEADS, SB_HEAD_DIM, PAGE_SIZE), lambda s, pt: (layer, pt[s, p], 0, 0, 0))

    grid_spec = pltpu.PrefetchScalarGridSpec(
        num_scalar_prefetch=1,
        grid=(n_seq,),
        in_specs=[tok, tok, tok, _resident((rows, PAGE_SIZE)), _resident((PAGE_SIZE, PAGE_SIZE))]
                 + [page(p) for p in range(n_pages)] * 2,
        out_specs=tok,
        scratch_shapes=[pltpu.VMEM((PAGE_SIZE, SB_WIDTH), F32), pltpu.VMEM((PAGE_SIZE, SB_WIDTH), F32)],
    )
    return pl.pallas_call(
        functools.partial(_sb_decode_body, n_pages=n_pages, n_new=n_new),
        grid_spec=grid_spec,
        out_shape=jax.ShapeDtypeStruct((n_seq, n_new, SB_WIDTH), F32),
        compiler_params=_params("arbitrary"),
        name="sb_decode",
    )(page_table, qb, kb, vb, bias_rows, tri, *([cache_kt] * n_pages), *([cache_vt] * n_pages))


def _conv_finish(y, cb, lg, lb):
    y = y + cb
    mu = jnp.mean(y, axis=-1, keepdims=True)
    yc = y - mu
    var = jnp.mean(yc * yc, axis=-1, keepdims=True)
    y = yc * lax.rsqrt(var + EPS) * lg + lb
    return y * _sigmoid(y)


def _conv_prompt_body(u_ref, prev_ref, w_ref, cb_ref, lg_ref, lb_ref, o_ref, buf, *, tile, chunk):
    i = pl.program_id(1)
    prev = prev_ref[...]
    buf[0, 0:CONV_HALO, :] = jnp.where(i > 0, prev, jnp.zeros_like(prev))
    buf[0, CONV_HALO:CONV_HALO + tile, :] = u_ref[...]
    span = CONV_HALO + tile - SUBLANES
    for s in range(1, SUBLANES):
        buf[s, 0:span, :] = buf[0, s:s + span, :]
    base = CONV_HALO - (CONV_W - 1)
    for r0 in range(0, tile, chunk):
        acc = jnp.zeros((chunk, C_CH), F32)
        for t in range(CONV_W):
            off = r0 + base + t
            lo = off - off % SUBLANES
            acc = acc + w_ref[t:t + 1, :] * buf[off % SUBLANES, lo:lo + chunk, :]
        o_ref[r0:r0 + chunk, :] = _conv_finish(acc, cb_ref[...], lg_ref[...], lb_ref[...]).astype(BF16)


def _conv_prompt(u, w, cb, lg, lb, batch, tile=256, chunk=32):
    n = u.shape[0]
    nt = n // batch // tile
    per = tile // CONV_HALO
    cur = pl.BlockSpec((tile, C_CH), lambda b, i: (b * nt + i, 0))
    prev = pl.BlockSpec((CONV_HALO, C_CH), lambda b, i: (jnp.maximum((b * nt + i) * per - 1, 0), 0))
    vec = _resident((1, C_CH))
    return pl.pallas_call(
        functools.partial(_conv_prompt_body, tile=tile, chunk=chunk),
        grid=(batch, nt),
        in_specs=[cur, prev, _resident((CONV_W, C_CH)), vec, vec, vec],
        out_specs=cur,
        out_shape=jax.ShapeDtypeStruct((n, C_CH), BF16),
        scratch_shapes=[pltpu.VMEM((SUBLANES, CONV_HALO + tile, C_CH), F32)],
        compiler_params=_params("parallel", "parallel"),
        name="conv_prompt",
    )(u, u, w, cb, lg, lb)


def _conv_decode_body(u_ref, st_ref, w_ref, cb_ref, lg_ref, lb_ref, o_ref, so_ref, *, n_new):
    ctx = CONV_W - 1
    full = lambda j: st_ref[j] if j < ctx else u_ref[j - ctx]
    for t in range(n_new):
        acc = w_ref[0:1, :] * full(t)
        for j in range(1, CONV_W):
            acc = acc + w_ref[j:j + 1, :] * full(t + j)
        o_ref[t] = _conv_finish(acc, cb_ref[...], lg_ref[...], lb_ref[...])
    for j in range(ctx):
        so_ref[j] = full(j + n_new)


def _conv_decode(u, state, layer, w, cb, lg, lb, per_step=16):
    n_new, n_seq, _ = u.shape
    ctx = CONV_W - 1
    ublk = pl.BlockSpec((n_new, per_step, C_CH), lambda i: (0, i, 0))
    vec = _resident((1, C_CH))
    return pl.pallas_call(
        functools.partial(_conv_decode_body, n_new=n_new),
        grid=(n_seq // per_step,),
        in_specs=[ublk, pl.BlockSpec((None, ctx, per_step, C_CH), lambda i: (layer, 0, i, 0)),
                  _resident((CONV_W, C_CH)), vec, vec, vec],
        out_specs=[ublk, pl.BlockSpec((ctx, per_step, C_CH), lambda i: (0, i, 0))],
        out_shape=[jax.ShapeDtypeStruct((n_new, n_seq, C_CH), F32), jax.ShapeDtypeStruct((ctx, n_seq, C_CH), F32)],
        compiler_params=_params("parallel"),
        name="conv_decode",
    )(u, state, w, cb, lg, lb)


def _mix_ffn_body(x_ref, oa_ref, ob_ref, oc_ref, gt_ref, wpa_ref, wpb_ref, wpc_ref, wo_ref, gm_ref, g1_ref, w1_ref,
                  w2_ref, g2_ref, y_ref, *, chunk):
    gate = lambda j: gt_ref[:, j * D_MODEL:(j + 1) * D_MODEL].astype(F32)
    m = (gate(0) * _dot(oa_ref[...], wpa_ref[...]) + gate(1) * _dot(ob_ref[...], wpb_ref[...])
         + gate(2) * _dot(oc_ref[...], wpc_ref[...]))
    x = x_ref[...] + _rms(_dot(m.astype(BF16), wo_ref[...]), gm_ref[...])
    h = _rms(x, g1_ref[...]).astype(BF16)
    acc = None
    for c0 in range(0, D_FF, chunk):
        a = jnp.maximum(_dot(h, w1_ref[:, c0:c0 + chunk]), 0.0)
        part = _dot((a * a).astype(BF16), w2_ref[c0:c0 + chunk, :])
        acc = part if acc is None else acc + part
    y_ref[...] = x + _rms(acc, g2_ref[...])


def _mix_ffn(x, oa, ob, oc, gt, wpa, wpb, wpc, wo, gm, g1, w1, w2, g2, tm, chunk=1024):
    n = x.shape[0]
    row = lambda width: pl.BlockSpec((tm, width), lambda i: (i, 0))
    vec = _resident((1, D_MODEL))
    return pl.pallas_call(
        functools.partial(_mix_ffn_body, chunk=chunk),
        grid=(n // tm,),
        in_specs=[row(D_MODEL), row(A_WIDTH), row(SB_WIDTH), row(C_CH), row(N_BRANCH * D_MODEL),
                  _resident(wpa.shape), _resident(wpb.shape), _resident(wpc.shape), _resident(wo.shape), vec,
                  vec, _resident(w1.shape), _resident(w2.shape), vec],
        out_specs=row(D_MODEL),
        out_shape=jax.ShapeDtypeStruct((n, D_MODEL), F32),
        compiler_params=_params("parallel"),
        name="mix_ffn",
    )(x, oa, ob, oc, gt, wpa, wpb, wpc, wo, gm, g1, w1, w2, g2)


def _trunk(x3, wts, lb, cache):
    batch, seq, _ = x3.shape
    n = batch * seq
    x = x3.reshape(n, D_MODEL)
    tm = SB_TILE
    depth = wts["w_in"].shape[0]
    ks, vs, ss, cs = [], [], [], []
    kv_stack = s_stack = None
    for l in range(depth):
        vec = lambda name: wts[name][l].reshape(1, -1)
        conv_args = (wts["conv_w"][l], vec("conv_b"), vec("ln_g"), vec("ln_b"))
        proj_args = (x, vec("g_pre_mix"), wts["w_in"][l], lb[l].reshape(1, -1), l, tm)
        nbias = -LOG2E * wts["sb_bias"][l]
        if cache is None:
            qa, lf, ia, ga, qb, u, gt, kt, vt, kth, vth = _in_proj(*proj_args, batch=batch, depth=depth,
                                                                   kv_stack=kv_stack)
            kv_stack = (kt, vt)
            oa, s_new = _hgrn_prompt(qa, lf, ia, ga, vec("a_norm"), batch)
            ss.append(s_new)
            ob = _sb_prompt(qb.reshape(batch, seq, SB_WIDTH), kth, vth, nbias).reshape(n, SB_WIDTH)
            oc = _conv_prompt(u, *conv_args, batch)
            cs.append(u.reshape(batch, seq, C_CH)[:, seq - (CONV_W - 1):])
        else:
            cache_kt, cache_vt, page_table, state_hgrn, state_conv_t = cache
            qa, lf, ia, ga, qb, u, gt, kb, vb = _in_proj(*proj_args)
            oa, s_stack = _hgrn_decode(qa, lf, ia, ga, state_hgrn, l, vec("a_norm"), seq, s_stack)
            bias_rows = jnp.broadcast_to(jnp.tile(nbias, seq)[:, None], (seq * SB_HEADS, PAGE_SIZE))
            tok = lambda a: a.reshape(batch, seq, -1)
            ob = _sb_decode(tok(qb).astype(F32), tok(kb), tok(vb), cache_kt, cache_vt, l, page_table, bias_rows)
            ob = ob.reshape(n, SB_WIDTH).astype(BF16)
            oc, c_new = _conv_decode(tok(u).transpose(1, 0, 2), state_conv_t, l, *conv_args)
            oc = oc.transpose(1, 0, 2).reshape(n, C_CH).astype(BF16)
            cs.append(c_new.transpose(1, 0, 2))
            ks.append(kb.reshape(batch, seq, SB_HEADS, SB_HEAD_DIM))
            vs.append(vb.reshape(batch, seq, SB_HEADS, SB_HEAD_DIM))
        x = _mix_ffn(x, oa, ob, oc, gt, wts["w_pa"][l], wts["w_pb"][l], wts["w_pc"][l], wts["w_o"][l],
                     vec("g_post_mix"), vec("g_pre_ffn"), wts["w_ff1"][l], wts["w_ff2"][l], vec("g_post_ffn"), tm)
    if cache is None:
        k_all, v_all = (a.reshape(depth, batch, SB_HEADS, SB_HEAD_DIM, seq).transpose(0, 1, 4, 2, 3) for a in kv_stack)
        s_all = jnp.stack(ss)
    else:
        k_all, v_all, s_all = jnp.stack(ks), jnp.stack(vs), s_stack
    return x.reshape(batch, seq, D_MODEL), k_all, v_all, s_all, jnp.stack(cs)


def kernel(x_prompt, x_sample, cache_k, cache_v, page_table, state_hgrn, state_conv, g_pre_mix, g_post_mix,
           g_pre_ffn, g_post_ffn, w_in, hgrn_lb, a_norm, w_pa, w_pb, sb_bias, conv_w, conv_b, ln_g, ln_b, w_pc,
           w_o, w_ff1, w_ff2):
    wts = dict(g_pre_mix=g_pre_mix, g_post_mix=g_post_mix, g_pre_ffn=g_pre_ffn, g_post_ffn=g_post_ffn,
               a_norm=a_norm, sb_bias=sb_bias, conv_w=conv_w, conv_b=conv_b, ln_g=ln_g, ln_b=ln_b,
               w_in=w_in.astype(BF16), w_pa=w_pa.astype(BF16), w_pb=w_pb.astype(BF16), w_pc=w_pc.astype(BF16),
               w_o=w_o.astype(BF16), w_ff1=w_ff1.astype(BF16), w_ff2=w_ff2.astype(BF16))
    lb = jnp.cumsum(jax.nn.softmax(hgrn_lb.astype(F32), axis=0), axis=0)
    lb = lb - lb[0:1]
    y_p, k_p, v_p, s_p, c_p = _trunk(x_prompt, wts, lb, None)
    cache = (cache_k.transpose(0, 1, 3, 4, 2), cache_v.transpose(0, 1, 3, 4, 2), page_table, state_hgrn,
             state_conv.transpose(0, 2, 1, 3))
    y_s, k_s, v_s, s_s, c_s = _trunk(x_sample, wts, lb, cache)
    return (y_p, y_s, k_p, v_p, s_p, c_p, k_s, v_s, s_s, c_s)
```

```python
import functools

import numpy as np
import jax
import jax.numpy as jnp
from jax import lax
from jax.experimental import pallas as pl
from jax.experimental.pallas import tpu as pltpu

F32 = jnp.float32
BF16 = jnp.bfloat16

D_MODEL = 1024
A_HEADS = 4
A_DK = 128
A_DV = 128
A_WIDTH = A_HEADS * A_DK
SB_HEADS = 8
SB_HEAD_DIM = 64
SB_WIDTH = SB_HEADS * SB_HEAD_DIM
C_CH = 512
CONV_W = 31
D_FF = 4 * D_MODEL
N_BRANCH = 3
EPS = 1e-6
PAGE_SIZE = 128
KV_OFFSET = 4 * A_WIDTH + SB_WIDTH
LOG2E = 1.4426950408889634
SB_QSCALE = -LOG2E * SB_HEAD_DIM ** -0.5

LANES = 128
SUBLANES = 8
VMEM_LIMIT_BYTES = 56 * 1024 * 1024

HGRN_ROWS = 128
SB_TILE = 256
CONV_HALO = 32
MIX_FFN_ROWS = 512

_NT = (((1,), (1,)), ((), ()))


def _params(*semantics):
    return pltpu.CompilerParams(dimension_semantics=semantics, vmem_limit_bytes=VMEM_LIMIT_BYTES)


def _resident(shape):
    nd = len(shape)
    return pl.BlockSpec(shape, lambda *_: (0,) * nd, pipeline_mode=pl.Buffered(1))


def _sigmoid(x):
    return 1.0 / (1.0 + jnp.exp(-x))


def _softplus(x):
    return jnp.maximum(x, 0.0) + jnp.log1p(jnp.exp(-jnp.abs(x)))


def _rms(x, g):
    return x * lax.rsqrt(jnp.mean(x * x, axis=-1, keepdims=True) + EPS) * g


def _dot(a, b):
    return jnp.dot(a, b, preferred_element_type=F32)


def _in_proj_body(x_ref, g_ref, w_ref, lb_ref, *refs, layer0, transposed_kv, n_alias):
    if transposed_kv:
        wkv_ref = refs[0]
        qa_ref, lf_ref, ia_ref, ga_ref, qb_ref, u_ref, gt_ref, kt_ref, vt_ref, kth_ref, vth_ref = refs[1 + n_alias:]
    else:
        qa_ref, lf_ref, ia_ref, ga_ref, qb_ref, u_ref, gt_ref, kb_ref, vb_ref = refs
    h = _rms(x_ref[...], g_ref[...]).astype(BF16)

    def seg(lo, n):
        return _dot(h, w_ref[:, lo:lo + n])

    qa = seg(0, A_WIDTH)
    qa_ref[...] = qa * _sigmoid(qa)
    fa = seg(A_WIDTH, A_WIDTH)
    if layer0:
        lf_ref[...] = -_softplus(-fa)
    else:
        lb = lb_ref[...]
        lf_ref[...] = jnp.log(lb + (1.0 - lb) * _sigmoid(fa))
    ia_ref[...] = seg(2 * A_WIDTH, A_WIDTH)
    ga = seg(3 * A_WIDTH, A_WIDTH)
    ga_ref[...] = ga * _sigmoid(ga)
    qb_ref[...] = (seg(4 * A_WIDTH, SB_WIDTH) * SB_QSCALE).astype(BF16)
    if transposed_kv:
        kv = lax.dot_general(wkv_ref[...], h, _NT, preferred_element_type=F32)
        kt_ref[...] = kv[0:SB_WIDTH]
        vt_ref[...] = kv[SB_WIDTH:2 * SB_WIDTH]
        kth_ref[...] = kv[0:SB_WIDTH].astype(BF16)
        vth_ref[...] = kv[SB_WIDTH:2 * SB_WIDTH].astype(BF16)
    else:
        kb_ref[...] = seg(KV_OFFSET, SB_WIDTH)
        vb_ref[...] = seg(KV_OFFSET + SB_WIDTH, SB_WIDTH)
    o = KV_OFFSET + 2 * SB_WIDTH
    u_ref[...] = seg(o, C_CH) * _sigmoid(seg(o + C_CH, C_CH))
    o += 2 * C_CH
    for j in range(N_BRANCH):
        gt_ref[:, j * D_MODEL:(j + 1) * D_MODEL] = _sigmoid(seg(o + j * D_MODEL, D_MODEL)).astype(BF16)


def _in_proj(x, g, w, lb, layer, tm, batch=None, depth=None, kv_stack=None):
    n = x.shape[0]
    in_width = w.shape[1]
    row = lambda width: pl.BlockSpec((tm, width), lambda i: (i, 0))
    widths = [A_WIDTH] * 4 + [SB_WIDTH, C_CH, N_BRANCH * D_MODEL]
    dtypes = [F32] * 4 + [BF16, F32, BF16]
    in_specs = [row(D_MODEL), _resident((1, D_MODEL)), _resident((D_MODEL, in_width)), _resident((1, A_WIDTH))]
    out_specs = [row(wd) for wd in widths]
    out_shape = [jax.ShapeDtypeStruct((n, wd), dt) for wd, dt in zip(widths, dtypes)]
    args = [x, g, w, lb]
    aliases = {}
    if batch is None:
        out_specs += [row(SB_WIDTH)] * 2
        out_shape += [jax.ShapeDtypeStruct((n, SB_WIDTH), F32)] * 2
    else:
        nt = n // batch // tm
        in_specs.append(_resident((2 * SB_WIDTH, D_MODEL)))
        args.append(w[:, KV_OFFSET:KV_OFFSET + 2 * SB_WIDTH].T)
        if kv_stack is not None:
            aliases = {len(args): len(out_specs), len(args) + 1: len(out_specs) + 1}
            in_specs += [pl.BlockSpec(memory_space=pl.ANY)] * 2
            args += list(kv_stack)
        out_specs += [pl.BlockSpec((None, None, SB_WIDTH, tm), lambda i: (layer, i // nt, 0, i % nt))] * 2
        out_specs += [pl.BlockSpec((None, None, SB_WIDTH, tm), lambda i: (i // nt, i % nt, 0, 0))] * 2
        out_shape += [jax.ShapeDtypeStruct((depth, batch, SB_WIDTH, nt * tm), F32)] * 2
        out_shape += [jax.ShapeDtypeStruct((batch, nt, SB_WIDTH, tm), BF16)] * 2
    return pl.pallas_call(
        functools.partial(_in_proj_body, layer0=layer == 0, transposed_kv=batch is not None, n_alias=len(aliases)),
        grid=(n // tm,),
        in_specs=in_specs,
        out_specs=out_specs,
        out_shape=out_shape,
        input_output_aliases=aliases,
        compiler_params=_params("parallel"),
        name="in_proj",
    )(*args)


def _level_matrix(rows, seg_len, with_total):
    r = np.arange(rows)[:, None]
    u = np.arange(rows)[None, :]
    mats = []
    m = seg_len // 2
    while m >= 1:
        start = (r // (2 * m)) * (2 * m)
        bound = start + m - 1
        is_q = (r - start) >= m
        wq = is_q & (u > bound) & (u <= r)
        wk = (~is_q) & (u > r) & (u <= bound)
        mats.append(wq | wk)
        m //= 2
    seg = (r // seg_len) == (u // seg_len)
    mats.append(seg & (u <= r))
    if with_total:
        mats.append(seg)
    return np.concatenate(mats, axis=0).astype(np.float32)


def _num_levels(seg_len):
    return int(np.log2(seg_len))


def _level_exponents(w, g):
    hi = g.astype(BF16)
    lo = (g - hi.astype(F32)).astype(BF16)
    return _dot(w, hi) + _dot(w, lo)


def _hgrn_intra(q, k, v, e_of, seg_len):
    rows = q.shape[0]
    row = lax.broadcasted_iota(jnp.int32, (rows, rows), 0)
    col = lax.broadcasted_iota(jnp.int32, (rows, rows), 1)
    rowq = lax.broadcasted_iota(jnp.int32, q.shape, 0)
    att = None
    m = seg_len // 2
    lev = 0
    while m >= 1:
        ee = jnp.exp(e_of(lev))
        is_q = (rowq & (2 * m - 1)) >= m
        qt = jnp.where(is_q, q * ee, 0.0).astype(BF16)
        kt = jnp.where(is_q, 0.0, k * ee).astype(BF16)
        a = lax.dot_general(qt, kt, _NT, preferred_element_type=F32)
        if 2 * m < rows:
            sh = int(np.log2(2 * m))
            a = jnp.where((row >> sh) == (col >> sh), a, 0.0)
        att = a if att is None else att + a
        m //= 2
        lev += 1
    d = jnp.sum(q * k, axis=1, keepdims=True)
    return _dot(att.astype(BF16), v.astype(BF16)) + d * v


def _hgrn_finish(o, an, ga):
    return (_rms(o, an) * ga).astype(BF16)


def _hgrn_prompt_body(q_ref, lf_ref, v_ref, ga_ref, w_ref, an_ref, o_ref, s_ref, st_scr, e_scr):
    c = pl.program_id(1)
    rows = HGRN_ROWS
    nlev = _num_levels(rows)

    @pl.when(c == 0)
    def _():
        st_scr[...] = jnp.zeros_like(st_scr)

    n_chunks = q_ref.shape[0] // rows
    for c0 in range(n_chunks):
        e_scr[c0] = _level_exponents(w_ref[...], lf_ref[c0 * rows:(c0 + 1) * rows, :])

    for h in range(A_HEADS):
        sl = slice(h * A_DK, (h + 1) * A_DK)
        st = st_scr[h]
        for c0 in range(n_chunks):
            rs = slice(c0 * rows, (c0 + 1) * rows)
            q = q_ref[rs, sl]
            v = v_ref[rs, sl]
            k = 1.0 - jnp.exp(lf_ref[rs, sl])
            o = _hgrn_intra(q, k, v, lambda lev: e_scr[c0, lev * rows:(lev + 1) * rows, sl], rows)
            b = e_scr[c0, nlev * rows:(nlev + 1) * rows, sl]
            o = o + lax.dot_general((q * jnp.exp(b)).astype(BF16), st.astype(BF16), _NT, preferred_element_type=F32)
            bl = b[rows - 1:rows, :]
            kk = (k * jnp.exp(bl - b)).astype(BF16)
            st = st * jnp.exp(bl) + _dot(v.T.astype(BF16), kk)
            o_ref[rs, sl] = _hgrn_finish(o, an_ref[...], ga_ref[rs, sl])
        st_scr[h] = st

    @pl.when(c == pl.num_programs(1) - 1)
    def _():
        for h in range(A_HEADS):
            s_ref[h] = st_scr[h].T


def _hgrn_prompt(qa, lf, ia, ga, an, batch, chunks_per_step=4):
    n = qa.shape[0]
    rows = HGRN_ROWS
    nc = n // batch // (rows * chunks_per_step)
    w = jnp.asarray(_level_matrix(rows, rows, False), BF16)
    blk = pl.BlockSpec((rows * chunks_per_step, A_WIDTH), lambda b, c: (b * nc + c, 0))
    return pl.pallas_call(
        _hgrn_prompt_body,
        grid=(batch, nc),
        in_specs=[blk, blk, blk, blk, _resident(w.shape), _resident((1, A_DV))],
        out_specs=[blk, pl.BlockSpec((None, A_HEADS, A_DK, A_DV), lambda b, c: (b, 0, 0, 0))],
        out_shape=[jax.ShapeDtypeStruct((n, A_WIDTH), BF16),
                   jax.ShapeDtypeStruct((batch, A_HEADS, A_DK, A_DV), F32)],
        scratch_shapes=[pltpu.VMEM((A_HEADS, A_DV, A_DK), F32),
                        pltpu.VMEM((chunks_per_step,) + w.shape[:1] + (A_WIDTH,), F32)],
        compiler_params=_params("parallel", "arbitrary"),
        name="hgrn_prompt",
    )(qa, lf, ia, ga, w, an)


def _hgrn_decode_body(q_ref, lf_ref, v_ref, ga_ref, s_ref, w_ref, an_ref, *refs, seg_len, n_alias):
    o_ref, so_ref = refs[n_alias:]
    rows = q_ref.shape[0]
    nseq = rows // seg_len
    nlev = _num_levels(seg_len)
    g = lf_ref[...]
    e = _level_exponents(w_ref[...], g)
    q = q_ref[...]
    v = v_ref[...]
    k = 1.0 - jnp.exp(g)
    o = _hgrn_intra(q, k, v, lambda lev: e[lev * rows:(lev + 1) * rows, :], seg_len)
    b = e[nlev * rows:(nlev + 1) * rows, :]
    btot = e[(nlev + 1) * rows:(nlev + 2) * rows, :]
    qh = q * jnp.exp(b)
    kkt = (k * jnp.exp(btot - b)).T
    ebt = jnp.exp(btot).T
    vh = v.astype(BF16)
    row = lax.broadcasted_iota(jnp.int32, (rows, A_DK), 0)
    col = lax.broadcasted_iota(jnp.int32, (A_DK, rows), 1)
    sh = int(np.log2(seg_len))
    for s in range(nseq):
        st = s_ref[s]
        o = o + _dot(jnp.where((row >> sh) == s, qh, 0.0).astype(BF16), st.astype(BF16))
        decay = jnp.broadcast_to(ebt[:, s * seg_len:s * seg_len + 1], (A_DK, A_DV))
        so_ref[s] = decay * st + _dot(jnp.where((col >> sh) == s, kkt, 0.0).astype(BF16), vh)
    o_ref[...] = _hgrn_finish(o, an_ref[...], ga_ref[...])


def _hgrn_decode(qa, lf, ia, ga, state, layer, an, seg_len, new_state=None):
    n = qa.shape[0]
    rows = HGRN_ROWS
    nseq = rows // seg_len
    w = jnp.asarray(_level_matrix(rows, seg_len, True), BF16)
    blk = pl.BlockSpec((rows, A_DK), lambda i, h: (i, h))
    sblk = pl.BlockSpec((None, nseq, None, A_DK, A_DV), lambda i, h: (layer, i, h, 0, 0))
    in_specs = [blk, blk, blk, blk, sblk, _resident(w.shape), _resident((1, A_DV))]
    args = [qa, lf, ia, ga, state, w, an]
    aliases = {}
    if new_state is not None:
        aliases = {len(args): 1}
        in_specs.append(pl.BlockSpec(memory_space=pl.ANY))
        args.append(new_state)
    return pl.pallas_call(
        functools.partial(_hgrn_decode_body, seg_len=seg_len, n_alias=len(aliases)),
        grid=(n // rows, A_HEADS),
        in_specs=in_specs,
        out_specs=[blk, sblk],
        out_shape=[jax.ShapeDtypeStruct((n, A_WIDTH), BF16), jax.ShapeDtypeStruct(state.shape, F32)],
        input_output_aliases=aliases,
        compiler_params=_params("parallel", "parallel"),
        name="hgrn_decode",
    )(*args)


def _sb_scores(qn, k, nbias, mask, kv_transposed):
    if kv_transposed:
        zn = _dot(qn, k) + nbias
    else:
        zn = lax.dot_general(qn, k, _NT, preferred_element_type=F32) + nbias
    lr = jnp.minimum(zn, 0.0) - jnp.log(1.0 + jnp.exp2(-jnp.abs(zn))) * LOG2E
    if mask is not None:
        lr = jnp.where(mask, lr, 0.0)
    return zn, lr.astype(BF16)


def _sb_weights(scores, v, run, tri, mask, kv_transposed):
    zn, lr = scores
    tail = _dot(lr, tri)
    a = jnp.exp2(tail - zn + run)
    if mask is not None:
        a = jnp.where(mask, a, 0.0)
    a = a.astype(BF16)
    if kv_transposed:
        pv = lax.dot_general(a, v, _NT, preferred_element_type=F32)
    else:
        pv = _dot(a, v)
    return pv, run + tail[:, 0:1]


def _sb_tile(qn, k, v, nbias, run, tri, mask, kv_transposed):
    return _sb_weights(_sb_scores(qn, k, nbias, mask, kv_transposed), v, run, tri, mask, kv_transposed)


def _sb_prompt_body(bias_ref, q_ref, k_ref, v_ref, tri_ref, o_ref, acc_ref, *slot_refs):
    slots = (slot_refs[:2], slot_refs[2:])
    pair = pl.program_id(0)
    i = pl.program_id(1)
    t = SB_TILE
    batch = q_ref.shape[0]
    tri = tri_ref[...]
    lane = lax.broadcasted_iota(jnp.int32, (t, LANES), 1)
    row = lax.broadcasted_iota(jnp.int32, (t, t), 0)
    col = lax.broadcasted_iota(jnp.int32, (t, t), 1)
    causal = col < row
    chains = [(b, hh) for b in range(batch) for hh in range(2)]
    qn = [jnp.where((lane < SB_HEAD_DIM) == (hh == 0), q_ref[b], jnp.zeros((t, LANES), BF16)) for b, hh in chains]
    nbias = [bias_ref[2 * pair + hh] for _, hh in chains]

    def scores(j, slot, mask=None):
        for c, (b, _) in enumerate(chains):
            for ref, val in zip(slots[slot], _sb_scores(qn[c], k_ref[b, j], nbias[c], mask, True)):
                ref[c] = val

    def weights(j, slot, runs, mask=None):
        out = []
        for c, (b, _) in enumerate(chains):
            sc = tuple(ref[c] for ref in slots[slot])
            pv, run = _sb_weights(sc, v_ref[b, j], runs[c], tri, mask, True)
            if mask is None:
                acc_ref[c] += pv
            else:
                acc_ref[c] = pv
            out.append(run)
        return tuple(out)

    scores(i, 1, causal)
    scores(jnp.maximum(i - 1, 0), 0)
    runs = weights(i, 1, (jnp.zeros((t, 1), F32),) * len(chains), causal)

    @pl.when(i > 0)
    def _():
        def step(n, runs):
            j = i - 1 - 2 * n
            scores(j - 1, 1)
            runs = weights(j, 0, runs)
            scores(jnp.maximum(j - 2, 0), 0)
            return weights(j - 1, 1, runs)

        runs_ = lax.fori_loop(0, i // 2, step, runs)

        @pl.when(i % 2 == 1)
        def _():
            weights(0, 0, runs_)

    for b in range(batch):
        o_ref[b] = jnp.where(lane < SB_HEAD_DIM, acc_ref[2 * b], acc_ref[2 * b + 1]).astype(BF16)


def _lower_tri(n):
    r = np.arange(n)
    return jnp.asarray((r[:, None] >= r[None, :]).astype(np.float32), BF16)


def _sb_prompt(qb, kth, vth, bias):
    batch, nq, _, t = kth.shape
    assert t == SB_TILE
    nc = 2 * batch
    tri = _lower_tri(t)
    qblk = pl.BlockSpec((batch, t, LANES), lambda p, i: (0, i, p))
    kvblk = pl.BlockSpec((batch, nq, LANES, t), lambda p, i: (0, 0, p, 0))
    return pl.pallas_call(
        _sb_prompt_body,
        grid=(SB_WIDTH // LANES, nq),
        in_specs=[pl.BlockSpec(memory_space=pltpu.SMEM), qblk, kvblk, kvblk, _resident((t, t))],
        out_specs=qblk,
        out_shape=jax.ShapeDtypeStruct(qb.shape, BF16),
        scratch_shapes=[pltpu.VMEM((nc, t, LANES), F32)]
                       + [pltpu.VMEM((nc, t, t), F32), pltpu.VMEM((nc, t, t), BF16)] * 2,
        compiler_params=_params("parallel", "arbitrary"),
        name="sb_prompt",
    )(bias, qb, kth, vth, tri)


def _sb_decode_body(pt_ref, q_ref, kn_ref, vn_ref, bias_ref, tri_ref, *refs, n_pages, n_new):
    kp = refs[:n_pages]
    vp = refs[n_pages:2 * n_pages]
    o_ref = refs[2 * n_pages]
    kpad, vpad = refs[2 * n_pages + 1:]
    del pt_ref
    rows = n_new * SB_HEADS
    q = q_ref[...]
    qbd = jnp.concatenate([jnp.broadcast_to(q[t:t + 1, :], (SB_HEADS, SB_WIDTH)) for t in range(n_new)], axis=0)
    row = lax.broadcasted_iota(jnp.int32, (rows, SB_WIDTH), 0)
    lane = lax.broadcasted_iota(jnp.int32, (rows, SB_WIDTH), 1)
    own = (lane // SB_HEAD_DIM) == (row % SB_HEADS)
    qbd = jnp.where(own, qbd, 0.0).astype(BF16)
    bias = bias_ref[...]
    tri = tri_ref[...]

    kpad[...] = jnp.zeros_like(kpad)
    vpad[...] = jnp.zeros_like(vpad)
    kpad[0:n_new, :] = kn_ref[...]
    vpad[0:n_new, :] = vn_ref[...]
    mrow = lax.broadcasted_iota(jnp.int32, (rows, PAGE_SIZE), 0)
    mcol = lax.broadcasted_iota(jnp.int32, (rows, PAGE_SIZE), 1)
    new_mask = mcol < (mrow // SB_HEADS)
    page_t = lambda ref: ref[...].reshape(SB_WIDTH, PAGE_SIZE).astype(BF16)
    scores = [_sb_scores(qbd, kpad[...].astype(BF16), bias, new_mask, False)]
    scores += [_sb_scores(qbd, page_t(kp[p]), bias, None, True) for p in range(n_pages - 1, -1, -1)]
    tail = _dot(jnp.concatenate([lr for _, lr in scores], axis=0), tri)
    run = jnp.zeros((rows, 1), F32)
    acc = None
    for s, (zn, _) in enumerate(scores):
        tl = tail[s * rows:(s + 1) * rows]
        a = jnp.exp2(tl - zn + run)
        run = run + tl[:, 0:1]
        if s == 0:
            acc = _dot(jnp.where(new_mask, a, 0.0).astype(BF16), vpad[...].astype(BF16))
        else:
            acc = acc + lax.dot_general(a.astype(BF16), page_t(vp[n_pages - s]), _NT, preferred_element_type=F32)
    acc = jnp.where(own, acc, 0.0)
    o_ref[...] = jnp.sum(acc.reshape(n_new, SB_HEADS, SB_WIDTH), axis=1)


def _sb_decode(qb, kb, vb, cache_kt, cache_vt, layer, page_table, bias_rows):
    n_seq, n_new, _ = qb.shape
    n_pages = page_table.shape[1]
    rows = n_new * SB_HEADS
    tri = _lower_tri(PAGE_SIZE)
    tok = pl.BlockSpec((None, n_new, SB_WIDTH), lambda s, pt: (s, 0, 0))

    def page(p):
        return pl.BlockSpec((None, None, SB_HEADS, SB_HEAD_DIM, PAGE_SIZE), lambda s, pt: (layer, pt[s, p], 0, 0, 0))

    grid_spec = pltpu.PrefetchScalarGridSpec(
        num_scalar_prefetch=1,
        grid=(n_seq,),
        in_specs=[tok, tok, tok, _resident((rows, PAGE_SIZE)), _resident((PAGE_SIZE, PAGE_SIZE))]
                 + [page(p) for p in range(n_pages)] * 2,
        out_specs=tok,
        scratch_shapes=[pltpu.VMEM((PAGE_SIZE, SB_WIDTH), F32), pltpu.VMEM((PAGE_SIZE, SB_WIDTH), F32)],
    )
    return pl.pallas_call(
        functools.partial(_sb_decode_body, n_pages=n_pages, n_new=n_new),
        grid_spec=grid_spec,
        out_shape=jax.ShapeDtypeStruct((n_seq, n_new, SB_WIDTH), F32),
        compiler_params=_params("arbitrary"),
        name="sb_decode",
    )(page_table, qb, kb, vb, bias_rows, tri, *([cache_kt] * n_pages), *([cache_vt] * n_pages))


def _conv_finish(y, cb, lg, lb):
    y = y + cb
    mu = jnp.mean(y, axis=-1, keepdims=True)
    yc = y - mu
    var = jnp.mean(yc * yc, axis=-1, keepdims=True)
    y = yc * lax.rsqrt(var + EPS) * lg + lb
    return y * _sigmoid(y)


def _conv_prompt_body(u_ref, prev_ref, w_ref, cb_ref, lg_ref, lb_ref, o_ref, buf, *, tile, chunk):
    i = pl.program_id(1)
    prev = prev_ref[...]
    buf[0, 0:CONV_HALO, :] = jnp.where(i > 0, prev, jnp.zeros_like(prev))
    buf[0, CONV_HALO:CONV_HALO + tile, :] = u_ref[...]
    span = CONV_HALO + tile - SUBLANES
    for s in range(1, SUBLANES):
        buf[s, 0:span, :] = buf[0, s:s + span, :]
    base = CONV_HALO - (CONV_W - 1)
    for r0 in range(0, tile, chunk):
        acc = jnp.zeros((chunk, C_CH), F32)
        for t in range(CONV_W):
            off = r0 + base + t
            lo = off - off % SUBLANES
            acc = acc + w_ref[t:t + 1, :] * buf[off % SUBLANES, lo:lo + chunk, :]
        o_ref[r0:r0 + chunk, :] = _conv_finish(acc, cb_ref[...], lg_ref[...], lb_ref[...]).astype(BF16)


def _conv_prompt(u, w, cb, lg, lb, batch, tile=256, chunk=32):
    n = u.shape[0]
    nt = n // batch // tile
    per = tile // CONV_HALO
    cur = pl.BlockSpec((tile, C_CH), lambda b, i: (b * nt + i, 0))
    prev = pl.BlockSpec((CONV_HALO, C_CH), lambda b, i: (jnp.maximum((b * nt + i) * per - 1, 0), 0))
    vec = _resident((1, C_CH))
    return pl.pallas_call(
        functools.partial(_conv_prompt_body, tile=tile, chunk=chunk),
        grid=(batch, nt),
        in_specs=[cur, prev, _resident((CONV_W, C_CH)), vec, vec, vec],
        out_specs=cur,
        out_shape=jax.ShapeDtypeStruct((n, C_CH), BF16),
        scratch_shapes=[pltpu.VMEM((SUBLANES, CONV_HALO + tile, C_CH), F32)],
        compiler_params=_params("parallel", "parallel"),
        name="conv_prompt",
    )(u, u, w, cb, lg, lb)


def _conv_decode_body(u_ref, st_ref, w_ref, cb_ref, lg_ref, lb_ref, o_ref, so_ref, *, n_new):
    ctx = CONV_W - 1
    full = lambda j: st_ref[j] if j < ctx else u_ref[j - ctx]
    for t in range(n_new):
        acc = w_ref[0:1, :] * full(t)
        for j in range(1, CONV_W):
            acc = acc + w_ref[j:j + 1, :] * full(t + j)
        o_ref[t] = _conv_finish(acc, cb_ref[...], lg_ref[...], lb_ref[...])
    for j in range(ctx):
        so_ref[j] = full(j + n_new)


def _conv_decode(u, state, layer, w, cb, lg, lb, per_step=16):
    n_new, n_seq, _ = u.shape
    ctx = CONV_W - 1
    ublk = pl.BlockSpec((n_new, per_step, C_CH), lambda i: (0, i, 0))
    vec = _resident((1, C_CH))
    return pl.pallas_call(
        functools.partial(_conv_decode_body, n_new=n_new),
        grid=(n_seq // per_step,),
        in_specs=[ublk, pl.BlockSpec((None, ctx, per_step, C_CH), lambda i: (layer, 0, i, 0)),
                  _resident((CONV_W, C_CH)), vec, vec, vec],
        out_specs=[ublk, pl.BlockSpec((ctx, per_step, C_CH), lambda i: (0, i, 0))],
        out_shape=[jax.ShapeDtypeStruct((n_new, n_seq, C_CH), F32), jax.ShapeDtypeStruct((ctx, n_seq, C_CH), F32)],
        compiler_params=_params("parallel"),
        name="conv_decode",
    )(u, state, w, cb, lg, lb)


def _mix_ffn_body(x_ref, oa_ref, ob_ref, oc_ref, gt_ref, wpa_ref, wpb_ref, wpc_ref, wo_ref, gm_ref, g1_ref, w1_ref,
                  w2_ref, g2_ref, y_ref, *, chunk):
    gate = lambda j: gt_ref[:, j * D_MODEL:(j + 1) * D_MODEL].astype(F32)
    m = (gate(0) * _dot(oa_ref[...], wpa_ref[...]) + gate(1) * _dot(ob_ref[...], wpb_ref[...])
         + gate(2) * _dot(oc_ref[...], wpc_ref[...]))
    x = x_ref[...] + _rms(_dot(m.astype(BF16), wo_ref[...]), gm_ref[...])
    h = _rms(x, g1_ref[...]).astype(BF16)
    acc = None
    for c0 in range(0, D_FF, chunk):
        a = jnp.maximum(_dot(h, w1_ref[:, c0:c0 + chunk]), 0.0)
        part = _dot((a * a).astype(BF16), w2_ref[c0:c0 + chunk, :])
        acc = part if acc is None else acc + part
    y_ref[...] = x + _rms(acc, g2_ref[...])


def _mix_ffn(x, oa, ob, oc, gt, wpa, wpb, wpc, wo, gm, g1, w1, w2, g2, tm, chunk=1024):
    n = x.shape[0]
    row = lambda width: pl.BlockSpec((tm, width), lambda i: (i, 0))
    vec = _resident((1, D_MODEL))
    return pl.pallas_call(
        functools.partial(_mix_ffn_body, chunk=chunk),
        grid=(n // tm,),
        in_specs=[row(D_MODEL), row(A_WIDTH), row(SB_WIDTH), row(C_CH), row(N_BRANCH * D_MODEL),
                  _resident(wpa.shape), _resident(wpb.shape), _resident(wpc.shape), _resident(wo.shape), vec,
                  vec, _resident(w1.shape), _resident(w2.shape), vec],
        out_specs=row(D_MODEL),
        out_shape=jax.ShapeDtypeStruct((n, D_MODEL), F32),
        compiler_params=_params("parallel"),
        name="mix_ffn",
    )(x, oa, ob, oc, gt, wpa, wpb, wpc, wo, gm, g1, w1, w2, g2)


def _trunk(x3, wts, lb, cache):
    batch, seq, _ = x3.shape
    n = batch * seq
    x = x3.reshape(n, D_MODEL)
    tm = SB_TILE
    depth = wts["w_in"].shape[0]
    ks, vs, ss, cs = [], [], [], []
    kv_stack = s_stack = None
    for l in range(depth):
        vec = lambda name: wts[name][l].reshape(1, -1)
        conv_args = (wts["conv_w"][l], vec("conv_b"), vec("ln_g"), vec("ln_b"))
        proj_args = (x, vec("g_pre_mix"), wts["w_in"][l], lb[l].reshape(1, -1), l, tm)
        nbias = -LOG2E * wts["sb_bias"][l]
        if cache is None:
            qa, lf, ia, ga, qb, u, gt, kt, vt, kth, vth = _in_proj(*proj_args, batch=batch, depth=depth,
                                                                   kv_stack=kv_stack)
            kv_stack = (kt, vt)
            oa, s_new = _hgrn_prompt(qa, lf, ia, ga, vec("a_norm"), batch)
            ss.append(s_new)
            ob = _sb_prompt(qb.reshape(batch, seq, SB_WIDTH), kth, vth, nbias).reshape(n, SB_WIDTH)
            oc = _conv_prompt(u, *conv_args, batch)
            cs.append(u.reshape(batch, seq, C_CH)[:, seq - (CONV_W - 1):])
        else:
            cache_kt, cache_vt, page_table, state_hgrn, state_conv_t = cache
            qa, lf, ia, ga, qb, u, gt, kb, vb = _in_proj(*proj_args)
            oa, s_stack = _hgrn_decode(qa, lf, ia, ga, state_hgrn, l, vec("a_norm"), seq, s_stack)
            bias_rows = jnp.broadcast_to(jnp.tile(nbias, seq)[:, None], (seq * SB_HEADS, PAGE_SIZE))
            tok = lambda a: a.reshape(batch, seq, -1)
            ob = _sb_decode(tok(qb).astype(F32), tok(kb), tok(vb), cache_kt, cache_vt, l, page_table, bias_rows)
            ob = ob.reshape(n, SB_WIDTH).astype(BF16)
            oc, c_new = _conv_decode(tok(u).transpose(1, 0, 2), state_conv_t, l, *conv_args)
            oc = oc.transpose(1, 0, 2).reshape(n, C_CH).astype(BF16)
            cs.append(c_new.transpose(1, 0, 2))
            ks.append(kb.reshape(batch, seq, SB_HEADS, SB_HEAD_DIM))
            vs.append(vb.reshape(batch, seq, SB_HEADS, SB_HEAD_DIM))
        x = _mix_ffn(x, oa, ob, oc, gt, wts["w_pa"][l], wts["w_pb"][l], wts["w_pc"][l], wts["w_o"][l],
                     vec("g_post_mix"), vec("g_pre_ffn"), wts["w_ff1"][l], wts["w_ff2"][l], vec("g_post_ffn"),
                     MIX_FFN_ROWS)
    if cache is None:
        k_all, v_all = (a.reshape(depth, batch, SB_HEADS, SB_HEAD_DIM, seq).transpose(0, 1, 4, 2, 3) for a in kv_stack)
        s_all = jnp.stack(ss)
    else:
        k_all, v_all, s_all = jnp.stack(ks), jnp.stack(vs), s_stack
    return x.reshape(batch, seq, D_MODEL), k_all, v_all, s_all, jnp.stack(cs)


def kernel(x_prompt, x_sample, cache_k, cache_v, page_table, state_hgrn, state_conv, g_pre_mix, g_post_mix,
           g_pre_ffn, g_post_ffn, w_in, hgrn_lb, a_norm, w_pa, w_pb, sb_bias, conv_w, conv_b, ln_g, ln_b, w_pc,
           w_o, w_ff1, w_ff2):
    wts = dict(g_pre_mix=g_pre_mix, g_post_mix=g_post_mix, g_pre_ffn=g_pre_ffn, g_post_ffn=g_post_ffn,
               a_norm=a_norm, sb_bias=sb_bias, conv_w=conv_w, conv_b=conv_b, ln_g=ln_g, ln_b=ln_b,
               w_in=w_in.astype(BF16), w_pa=w_pa.astype(BF16), w_pb=w_pb.astype(BF16), w_pc=w_pc.astype(BF16),
               w_o=w_o.astype(BF16), w_ff1=w_ff1.astype(BF16), w_ff2=w_ff2.astype(BF16))
    lb = jnp.cumsum(jax.nn.softmax(hgrn_lb.astype(F32), axis=0), axis=0)
    lb = lb - lb[0:1]
    y_p, k_p, v_p, s_p, c_p = _trunk(x_prompt, wts, lb, None)
    cache = (cache_k.transpose(0, 1, 3, 4, 2), cache_v.transpose(0, 1, 3, 4, 2), page_table, state_hgrn,
             state_conv.transpose(0, 2, 1, 3))
    y_s, k_s, v_s, s_s, c_s = _trunk(x_sample, wts, lb, cache)
    return (y_p, y_s, k_p, v_p, s_p, c_p, k_s, v_s, s_s, c_s)
```

```python
import functools

import numpy as np
import jax
import jax.numpy as jnp
from jax import lax
from jax.experimental import pallas as pl
from jax.experimental.pallas import tpu as pltpu

F32 = jnp.float32
BF16 = jnp.bfloat16

D_MODEL = 1024
A_HEADS = 4
A_DK = 128
A_DV = 128
A_WIDTH = A_HEADS * A_DK
SB_HEADS = 8
SB_HEAD_DIM = 64
SB_WIDTH = SB_HEADS * SB_HEAD_DIM
C_CH = 512
CONV_W = 31
D_FF = 4 * D_MODEL
N_BRANCH = 3
EPS = 1e-6
PAGE_SIZE = 128
KV_OFFSET = 4 * A_WIDTH + SB_WIDTH
LOG2E = 1.4426950408889634
SB_QSCALE = -LOG2E * SB_HEAD_DIM ** -0.5

LANES = 128
SUBLANES = 8
VMEM_LIMIT_BYTES = 56 * 1024 * 1024

HGRN_ROWS = 128
SB_TILE = 256
CONV_HALO = 32
MIX_FFN_ROWS = 512

_NT = (((1,), (1,)), ((), ()))


def _params(*semantics):
    return pltpu.CompilerParams(dimension_semantics=semantics, vmem_limit_bytes=VMEM_LIMIT_BYTES)


def _resident(shape):
    nd = len(shape)
    return pl.BlockSpec(shape, lambda *_: (0,) * nd, pipeline_mode=pl.Buffered(1))


def _sigmoid(x):
    return 1.0 / (1.0 + jnp.exp(-x))


def _softplus(x):
    return jnp.maximum(x, 0.0) + jnp.log1p(jnp.exp(-jnp.abs(x)))


def _rms(x, g):
    return x * lax.rsqrt(jnp.mean(x * x, axis=-1, keepdims=True) + EPS) * g


def _dot(a, b):
    return jnp.dot(a, b, preferred_element_type=F32)


def _in_proj_body(x_ref, g_ref, w_ref, lb_ref, *refs, layer0, transposed_kv, n_alias):
    if transposed_kv:
        wkv_ref = refs[0]
        qa_ref, lf_ref, ia_ref, ga_ref, qb_ref, u_ref, gt_ref, kt_ref, vt_ref, kth_ref, vth_ref = refs[1 + n_alias:]
    else:
        qa_ref, lf_ref, ia_ref, ga_ref, qb_ref, u_ref, gt_ref, kb_ref, vb_ref = refs
    h = _rms(x_ref[...], g_ref[...]).astype(BF16)

    def seg(lo, n):
        return _dot(h, w_ref[:, lo:lo + n])

    qa = seg(0, A_WIDTH)
    qa_ref[...] = qa * _sigmoid(qa)
    fa = seg(A_WIDTH, A_WIDTH)
    if layer0:
        lf_ref[...] = -_softplus(-fa)
    else:
        lb = lb_ref[...]
        lf_ref[...] = jnp.log(lb + (1.0 - lb) * _sigmoid(fa))
    ia_ref[...] = seg(2 * A_WIDTH, A_WIDTH)
    ga = seg(3 * A_WIDTH, A_WIDTH)
    ga_ref[...] = ga * _sigmoid(ga)
    qb_ref[...] = (seg(4 * A_WIDTH, SB_WIDTH) * SB_QSCALE).astype(BF16)
    if transposed_kv:
        kv = lax.dot_general(wkv_ref[...], h, _NT, preferred_element_type=F32)
        kt_ref[...] = kv[0:SB_WIDTH]
        vt_ref[...] = kv[SB_WIDTH:2 * SB_WIDTH]
        kth_ref[...] = kv[0:SB_WIDTH].astype(BF16)
        vth_ref[...] = kv[SB_WIDTH:2 * SB_WIDTH].astype(BF16)
    else:
        kb_ref[...] = seg(KV_OFFSET, SB_WIDTH)
        vb_ref[...] = seg(KV_OFFSET + SB_WIDTH, SB_WIDTH)
    o = KV_OFFSET + 2 * SB_WIDTH
    u_ref[...] = seg(o, C_CH) * _sigmoid(seg(o + C_CH, C_CH))
    o += 2 * C_CH
    for j in range(N_BRANCH):
        gt_ref[:, j * D_MODEL:(j + 1) * D_MODEL] = _sigmoid(seg(o + j * D_MODEL, D_MODEL)).astype(BF16)


def _in_proj(x, g, w, lb, layer, tm, batch=None, depth=None, kv_stack=None):
    n = x.shape[0]
    in_width = w.shape[1]
    row = lambda width: pl.BlockSpec((tm, width), lambda i: (i, 0))
    widths = [A_WIDTH] * 4 + [SB_WIDTH, C_CH, N_BRANCH * D_MODEL]
    dtypes = [F32] * 4 + [BF16, F32, BF16]
    in_specs = [row(D_MODEL), _resident((1, D_MODEL)), _resident((D_MODEL, in_width)), _resident((1, A_WIDTH))]
    out_specs = [row(wd) for wd in widths]
    out_shape = [jax.ShapeDtypeStruct((n, wd), dt) for wd, dt in zip(widths, dtypes)]
    args = [x, g, w, lb]
    aliases = {}
    if batch is None:
        out_specs += [row(SB_WIDTH)] * 2
        out_shape += [jax.ShapeDtypeStruct((n, SB_WIDTH), F32)] * 2
    else:
        nt = n // batch // tm
        in_specs.append(_resident((2 * SB_WIDTH, D_MODEL)))
        args.append(w[:, KV_OFFSET:KV_OFFSET + 2 * SB_WIDTH].T)
        if kv_stack is not None:
            aliases = {len(args): len(out_specs), len(args) + 1: len(out_specs) + 1}
            in_specs += [pl.BlockSpec(memory_space=pl.ANY)] * 2
            args += list(kv_stack)
        out_specs += [pl.BlockSpec((None, None, SB_WIDTH, tm), lambda i: (layer, i // nt, 0, i % nt))] * 2
        out_specs += [pl.BlockSpec((None, None, SB_WIDTH, tm), lambda i: (i // nt, i % nt, 0, 0))] * 2
        out_shape += [jax.ShapeDtypeStruct((depth, batch, SB_WIDTH, nt * tm), F32)] * 2
        out_shape += [jax.ShapeDtypeStruct((batch, nt, SB_WIDTH, tm), BF16)] * 2
    return pl.pallas_call(
        functools.partial(_in_proj_body, layer0=layer == 0, transposed_kv=batch is not None, n_alias=len(aliases)),
        grid=(n // tm,),
        in_specs=in_specs,
        out_specs=out_specs,
        out_shape=out_shape,
        input_output_aliases=aliases,
        compiler_params=_params("parallel"),
        name="in_proj",
    )(*args)


def _level_matrix(rows, seg_len, with_total):
    r = np.arange(rows)[:, None]
    u = np.arange(rows)[None, :]
    mats = []
    m = seg_len // 2
    while m >= 1:
        start = (r // (2 * m)) * (2 * m)
        bound = start + m - 1
        is_q = (r - start) >= m
        wq = is_q & (u > bound) & (u <= r)
        wk = (~is_q) & (u > r) & (u <= bound)
        mats.append(wq | wk)
        m //= 2
    seg = (r // seg_len) == (u // seg_len)
    mats.append(seg & (u <= r))
    if with_total:
        mats.append(seg)
    return np.concatenate(mats, axis=0).astype(np.float32)


def _num_levels(seg_len):
    return int(np.log2(seg_len))


def _level_exponents(w, g):
    hi = g.astype(BF16)
    lo = (g - hi.astype(F32)).astype(BF16)
    return _dot(w, hi) + _dot(w, lo)


def _hgrn_intra(q, k, v, e_of, seg_len):
    rows = q.shape[0]
    row = lax.broadcasted_iota(jnp.int32, (rows, rows), 0)
    col = lax.broadcasted_iota(jnp.int32, (rows, rows), 1)
    rowq = lax.broadcasted_iota(jnp.int32, q.shape, 0)
    att = None
    m = seg_len // 2
    lev = 0
    while m >= 1:
        ee = jnp.exp(e_of(lev))
        is_q = (rowq & (2 * m - 1)) >= m
        qt = jnp.where(is_q, q * ee, 0.0).astype(BF16)
        kt = jnp.where(is_q, 0.0, k * ee).astype(BF16)
        a = lax.dot_general(qt, kt, _NT, preferred_element_type=F32)
        if 2 * m < rows:
            sh = int(np.log2(2 * m))
            a = jnp.where((row >> sh) == (col >> sh), a, 0.0)
        att = a if att is None else att + a
        m //= 2
        lev += 1
    d = jnp.sum(q * k, axis=1, keepdims=True)
    return _dot(att.astype(BF16), v.astype(BF16)) + d * v


def _hgrn_finish(o, an, ga):
    return (_rms(o, an) * ga).astype(BF16)


def _hgrn_prompt_body(q_ref, lf_ref, v_ref, ga_ref, w_ref, an_ref, o_ref, s_ref, st_scr, e_scr):
    c = pl.program_id(1)
    rows = HGRN_ROWS
    nlev = _num_levels(rows)

    @pl.when(c == 0)
    def _():
        st_scr[...] = jnp.zeros_like(st_scr)

    n_chunks = q_ref.shape[0] // rows
    for c0 in range(n_chunks):
        e_scr[c0] = _level_exponents(w_ref[...], lf_ref[c0 * rows:(c0 + 1) * rows, :])

    for h in range(A_HEADS):
        sl = slice(h * A_DK, (h + 1) * A_DK)
        st = st_scr[h]
        for c0 in range(n_chunks):
            rs = slice(c0 * rows, (c0 + 1) * rows)
            q = q_ref[rs, sl]
            v = v_ref[rs, sl]
            k = 1.0 - jnp.exp(lf_ref[rs, sl])
            o = _hgrn_intra(q, k, v, lambda lev: e_scr[c0, lev * rows:(lev + 1) * rows, sl], rows)
            b = e_scr[c0, nlev * rows:(nlev + 1) * rows, sl]
            o = o + lax.dot_general((q * jnp.exp(b)).astype(BF16), st.astype(BF16), _NT, preferred_element_type=F32)
            bl = b[rows - 1:rows, :]
            kk = (k * jnp.exp(bl - b)).astype(BF16)
            st = st * jnp.exp(bl) + _dot(v.T.astype(BF16), kk)
            o_ref[rs, sl] = _hgrn_finish(o, an_ref[...], ga_ref[rs, sl])
        st_scr[h] = st

    @pl.when(c == pl.num_programs(1) - 1)
    def _():
        for h in range(A_HEADS):
            s_ref[h] = st_scr[h].T


def _hgrn_prompt(qa, lf, ia, ga, an, batch, chunks_per_step=4):
    n = qa.shape[0]
    rows = HGRN_ROWS
    nc = n // batch // (rows * chunks_per_step)
    w = jnp.asarray(_level_matrix(rows, rows, False), BF16)
    blk = pl.BlockSpec((rows * chunks_per_step, A_WIDTH), lambda b, c: (b * nc + c, 0))
    return pl.pallas_call(
        _hgrn_prompt_body,
        grid=(batch, nc),
        in_specs=[blk, blk, blk, blk, _resident(w.shape), _resident((1, A_DV))],
        out_specs=[blk, pl.BlockSpec((None, A_HEADS, A_DK, A_DV), lambda b, c: (b, 0, 0, 0))],
        out_shape=[jax.ShapeDtypeStruct((n, A_WIDTH), BF16),
                   jax.ShapeDtypeStruct((batch, A_HEADS, A_DK, A_DV), F32)],
        scratch_shapes=[pltpu.VMEM((A_HEADS, A_DV, A_DK), F32),
                        pltpu.VMEM((chunks_per_step,) + w.shape[:1] + (A_WIDTH,), F32)],
        compiler_params=_params("parallel", "arbitrary"),
        name="hgrn_prompt",
    )(qa, lf, ia, ga, w, an)


def _hgrn_decode_body(q_ref, lf_ref, v_ref, ga_ref, s_ref, w_ref, an_ref, *refs, seg_len, n_alias):
    o_ref, so_ref = refs[n_alias:]
    rows = q_ref.shape[0]
    nseq = rows // seg_len
    nlev = _num_levels(seg_len)
    g = lf_ref[...]
    e = _level_exponents(w_ref[...], g)
    q = q_ref[...]
    v = v_ref[...]
    k = 1.0 - jnp.exp(g)
    o = _hgrn_intra(q, k, v, lambda lev: e[lev * rows:(lev + 1) * rows, :], seg_len)
    b = e[nlev * rows:(nlev + 1) * rows, :]
    btot = e[(nlev + 1) * rows:(nlev + 2) * rows, :]
    qh = q * jnp.exp(b)
    kkt = (k * jnp.exp(btot - b)).T
    ebt = jnp.exp(btot).T
    vh = v.astype(BF16)
    row = lax.broadcasted_iota(jnp.int32, (rows, A_DK), 0)
    col = lax.broadcasted_iota(jnp.int32, (A_DK, rows), 1)
    sh = int(np.log2(seg_len))
    for s in range(nseq):
        st = s_ref[s]
        o = o + _dot(jnp.where((row >> sh) == s, qh, 0.0).astype(BF16), st.astype(BF16))
        decay = jnp.broadcast_to(ebt[:, s * seg_len:s * seg_len + 1], (A_DK, A_DV))
        so_ref[s] = decay * st + _dot(jnp.where((col >> sh) == s, kkt, 0.0).astype(BF16), vh)
    o_ref[...] = _hgrn_finish(o, an_ref[...], ga_ref[...])


def _hgrn_decode(qa, lf, ia, ga, state, layer, an, seg_len, new_state=None):
    n = qa.shape[0]
    rows = HGRN_ROWS
    nseq = rows // seg_len
    w = jnp.asarray(_level_matrix(rows, seg_len, True), BF16)
    blk = pl.BlockSpec((rows, A_DK), lambda i, h: (i, h))
    sblk = pl.BlockSpec((None, nseq, None, A_DK, A_DV), lambda i, h: (layer, i, h, 0, 0))
    in_specs = [blk, blk, blk, blk, sblk, _resident(w.shape), _resident((1, A_DV))]
    args = [qa, lf, ia, ga, state, w, an]
    aliases = {}
    if new_state is not None:
        aliases = {len(args): 1}
        in_specs.append(pl.BlockSpec(memory_space=pl.ANY))
        args.append(new_state)
    return pl.pallas_call(
        functools.partial(_hgrn_decode_body, seg_len=seg_len, n_alias=len(aliases)),
        grid=(n // rows, A_HEADS),
        in_specs=in_specs,
        out_specs=[blk, sblk],
        out_shape=[jax.ShapeDtypeStruct((n, A_WIDTH), BF16), jax.ShapeDtypeStruct(state.shape, F32)],
        input_output_aliases=aliases,
        compiler_params=_params("parallel", "parallel"),
        name="hgrn_decode",
    )(*args)


def _sb_scores(qn, k, nbias, mask, kv_transposed):
    if kv_transposed:
        zn = _dot(qn, k) + nbias
    else:
        zn = lax.dot_general(qn, k, _NT, preferred_element_type=F32) + nbias
    lr = jnp.minimum(zn, 0.0) - jnp.log(1.0 + jnp.exp2(-jnp.abs(zn))) * LOG2E
    if mask is not None:
        lr = jnp.where(mask, lr, 0.0)
    return zn, lr.astype(BF16)


def _sb_weights(scores, v, run, tri, mask, kv_transposed):
    zn, lr = scores
    tail = _dot(lr, tri)
    a = jnp.exp2(tail - zn + run)
    if mask is not None:
        a = jnp.where(mask, a, 0.0)
    a = a.astype(BF16)
    if kv_transposed:
        pv = lax.dot_general(a, v, _NT, preferred_element_type=F32)
    else:
        pv = _dot(a, v)
    return pv, run + tail[:, 0:1]


def _sb_tile(qn, k, v, nbias, run, tri, mask, kv_transposed):
    return _sb_weights(_sb_scores(qn, k, nbias, mask, kv_transposed), v, run, tri, mask, kv_transposed)


def _sb_prompt_body(bias_ref, q_ref, k_ref, v_ref, tri_ref, o_ref, acc_ref, *slot_refs):
    slots = (slot_refs[:2], slot_refs[2:])
    pair = pl.program_id(0)
    i = pl.program_id(1)
    t = SB_TILE
    batch = q_ref.shape[0]
    tri = tri_ref[...]
    lane = lax.broadcasted_iota(jnp.int32, (t, LANES), 1)
    row = lax.broadcasted_iota(jnp.int32, (t, t), 0)
    col = lax.broadcasted_iota(jnp.int32, (t, t), 1)
    causal = col < row
    chains = [(b, hh) for b in range(batch) for hh in range(2)]
    qn = [jnp.where((lane < SB_HEAD_DIM) == (hh == 0), q_ref[b], jnp.zeros((t, LANES), BF16)) for b, hh in chains]
    nbias = [bias_ref[2 * pair + hh] for _, hh in chains]

    def scores(j, slot, mask=None):
        for c, (b, _) in enumerate(chains):
            for ref, val in zip(slots[slot], _sb_scores(qn[c], k_ref[b, j], nbias[c], mask, True)):
                ref[c] = val

    def weights(j, slot, runs, mask=None):
        out = []
        for c, (b, _) in enumerate(chains):
            sc = tuple(ref[c] for ref in slots[slot])
            pv, run = _sb_weights(sc, v_ref[b, j], runs[c], tri, mask, True)
            if mask is None:
                acc_ref[c] += pv
            else:
                acc_ref[c] = pv
            out.append(run)
        return tuple(out)

    scores(i, 1, causal)
    scores(jnp.maximum(i - 1, 0), 0)
    runs = weights(i, 1, (jnp.zeros((t, 1), F32),) * len(chains), causal)

    @pl.when(i > 0)
    def _():
        def step2(j, runs):
            scores(j - 1, 1)
            runs = weights(j, 0, runs)
            scores(jnp.maximum(j - 2, 0), 0)
            return weights(j - 1, 1, runs)

        def step4(n, runs):
            j = i - 1 - 4 * n
            return step2(j - 2, step2(j, runs))

        runs4 = lax.fori_loop(0, i // 4, step4, runs)
        j_rem = i - 1 - 4 * (i // 4)
        runs_ = lax.fori_loop(0, (i % 4) // 2, lambda n, r: step2(j_rem, r), runs4)

        @pl.when(i % 2 == 1)
        def _():
            weights(0, 0, runs_)

    for b in range(batch):
        o_ref[b] = jnp.where(lane < SB_HEAD_DIM, acc_ref[2 * b], acc_ref[2 * b + 1]).astype(BF16)


def _lower_tri(n):
    r = np.arange(n)
    return jnp.asarray((r[:, None] >= r[None, :]).astype(np.float32), BF16)


def _sb_prompt(qb, kth, vth, bias):
    batch, nq, _, t = kth.shape
    assert t == SB_TILE
    nc = 2 * batch
    tri = _lower_tri(t)
    qblk = pl.BlockSpec((batch, t, LANES), lambda p, i: (0, i, p))
    kvblk = pl.BlockSpec((batch, nq, LANES, t), lambda p, i: (0, 0, p, 0))
    return pl.pallas_call(
        _sb_prompt_body,
        grid=(SB_WIDTH // LANES, nq),
        in_specs=[pl.BlockSpec(memory_space=pltpu.SMEM), qblk, kvblk, kvblk, _resident((t, t))],
        out_specs=qblk,
        out_shape=jax.ShapeDtypeStruct(qb.shape, BF16),
        scratch_shapes=[pltpu.VMEM((nc, t, LANES), F32)]
                       + [pltpu.VMEM((nc, t, t), F32), pltpu.VMEM((nc, t, t), BF16)] * 2,
        compiler_params=_params("parallel", "arbitrary"),
        name="sb_prompt",
    )(bias, qb, kth, vth, tri)


def _sb_decode_body(pt_ref, q_ref, kn_ref, vn_ref, bias_ref, tri_ref, *refs, n_pages, n_new):
    kp = refs[:n_pages]
    vp = refs[n_pages:2 * n_pages]
    o_ref = refs[2 * n_pages]
    kpad, vpad = refs[2 * n_pages + 1:]
    del pt_ref
    rows = n_new * SB_HEADS
    q = q_ref[...]
    qbd = jnp.concatenate([jnp.broadcast_to(q[t:t + 1, :], (SB_HEADS, SB_WIDTH)) for t in range(n_new)], axis=0)
    row = lax.broadcasted_iota(jnp.int32, (rows, SB_WIDTH), 0)
    lane = lax.broadcasted_iota(jnp.int32, (rows, SB_WIDTH), 1)
    own = (lane // SB_HEAD_DIM) == (row % SB_HEADS)
    qbd = jnp.where(own, qbd, 0.0).astype(BF16)
    bias = bias_ref[...]
    tri = tri_ref[...]

    kpad[...] = jnp.zeros_like(kpad)
    vpad[...] = jnp.zeros_like(vpad)
    kpad[0:n_new, :] = kn_ref[...]
    vpad[0:n_new, :] = vn_ref[...]
    mrow = lax.broadcasted_iota(jnp.int32, (rows, PAGE_SIZE), 0)
    mcol = lax.broadcasted_iota(jnp.int32, (rows, PAGE_SIZE), 1)
    new_mask = mcol < (mrow // SB_HEADS)
    page_t = lambda ref: ref[...].reshape(SB_WIDTH, PAGE_SIZE).astype(BF16)
    scores = [_sb_scores(qbd, kpad[...].astype(BF16), bias, new_mask, False)]
    scores += [_sb_scores(qbd, page_t(kp[p]), bias, None, True) for p in range(n_pages - 1, -1, -1)]
    tail = _dot(jnp.concatenate([lr for _, lr in scores], axis=0), tri)
    run = jnp.zeros((rows, 1), F32)
    acc = None
    for s, (zn, _) in enumerate(scores):
        tl = tail[s * rows:(s + 1) * rows]
        a = jnp.exp2(tl - zn + run)
        run = run + tl[:, 0:1]
        if s == 0:
            acc = _dot(jnp.where(new_mask, a, 0.0).astype(BF16), vpad[...].astype(BF16))
        else:
            acc = acc + lax.dot_general(a.astype(BF16), page_t(vp[n_pages - s]), _NT, preferred_element_type=F32)
    acc = jnp.where(own, acc, 0.0)
    o_ref[...] = jnp.sum(acc.reshape(n_new, SB_HEADS, SB_WIDTH), axis=1)


def _sb_decode(qb, kb, vb, cache_kt, cache_vt, layer, page_table, bias_rows):
    n_seq, n_new, _ = qb.shape
    n_pages = page_table.shape[1]
    rows = n_new * SB_HEADS
    tri = _lower_tri(PAGE_SIZE)
    tok = pl.BlockSpec((None, n_new, SB_WIDTH), lambda s, pt: (s, 0, 0))

    def page(p):
        return pl.BlockSpec((None, None, SB_HEADS, SB_HEAD_DIM, PAGE_SIZE), lambda s, pt: (layer, pt[s, p], 0, 0, 0))

    grid_spec = pltpu.PrefetchScalarGridSpec(
        num_scalar_prefetch=1,
        grid=(n_seq,),
        in_specs=[tok, tok, tok, _resident((rows, PAGE_SIZE)), _resident((PAGE_SIZE, PAGE_SIZE))]
                 + [page(p) for p in range(n_pages)] * 2,
        out_specs=tok,
        scratch_shapes=[pltpu.VMEM((PAGE_SIZE, SB_WIDTH), F32), pltpu.VMEM((PAGE_SIZE, SB_WIDTH), F32)],
    )
    return pl.pallas_call(
        functools.partial(_sb_decode_body, n_pages=n_pages, n_new=n_new),
        grid_spec=grid_spec,
        out_shape=jax.ShapeDtypeStruct((n_seq, n_new, SB_WIDTH), F32),
        compiler_params=_params("arbitrary"),
        name="sb_decode",
    )(page_table, qb, kb, vb, bias_rows, tri, *([cache_kt] * n_pages), *([cache_vt] * n_pages))


def _conv_finish(y, cb, lg, lb):
    y = y + cb
    mu = jnp.mean(y, axis=-1, keepdims=True)
    yc = y - mu
    var = jnp.mean(yc * yc, axis=-1, keepdims=True)
    y = yc * lax.rsqrt(var + EPS) * lg + lb
    return y * _sigmoid(y)


def _conv_prompt_body(u_ref, prev_ref, w_ref, cb_ref, lg_ref, lb_ref, o_ref, buf, *, tile, chunk):
    i = pl.program_id(1)
    prev = prev_ref[...]
    buf[0, 0:CONV_HALO, :] = jnp.where(i > 0, prev, jnp.zeros_like(prev))
    buf[0, CONV_HALO:CONV_HALO + tile, :] = u_ref[...]
    span = CONV_HALO + tile - SUBLANES
    for s in range(1, SUBLANES):
        buf[s, 0:span, :] = buf[0, s:s + span, :]
    base = CONV_HALO - (CONV_W - 1)
    for r0 in range(0, tile, chunk):
        acc = jnp.zeros((chunk, C_CH), F32)
        for t in range(CONV_W):
            off = r0 + base + t
            lo = off - off % SUBLANES
            acc = acc + w_ref[t:t + 1, :] * buf[off % SUBLANES, lo:lo + chunk, :]
        o_ref[r0:r0 + chunk, :] = _conv_finish(acc, cb_ref[...], lg_ref[...], lb_ref[...]).astype(BF16)


def _conv_prompt(u, w, cb, lg, lb, batch, tile=256, chunk=32):
    n = u.shape[0]
    nt = n // batch // tile
    per = tile // CONV_HALO
    cur = pl.BlockSpec((tile, C_CH), lambda b, i: (b * nt + i, 0))
    prev = pl.BlockSpec((CONV_HALO, C_CH), lambda b, i: (jnp.maximum((b * nt + i) * per - 1, 0), 0))
    vec = _resident((1, C_CH))
    return pl.pallas_call(
        functools.partial(_conv_prompt_body, tile=tile, chunk=chunk),
        grid=(batch, nt),
        in_specs=[cur, prev, _resident((CONV_W, C_CH)), vec, vec, vec],
        out_specs=cur,
        out_shape=jax.ShapeDtypeStruct((n, C_CH), BF16),
        scratch_shapes=[pltpu.VMEM((SUBLANES, CONV_HALO + tile, C_CH), F32)],
        compiler_params=_params("parallel", "parallel"),
        name="conv_prompt",
    )(u, u, w, cb, lg, lb)


def _conv_decode_body(u_ref, st_ref, w_ref, cb_ref, lg_ref, lb_ref, o_ref, so_ref, *, n_new):
    ctx = CONV_W - 1
    full = lambda j: st_ref[j] if j < ctx else u_ref[j - ctx]
    for t in range(n_new):
        acc = w_ref[0:1, :] * full(t)
        for j in range(1, CONV_W):
            acc = acc + w_ref[j:j + 1, :] * full(t + j)
        o_ref[t] = _conv_finish(acc, cb_ref[...], lg_ref[...], lb_ref[...])
    for j in range(ctx):
        so_ref[j] = full(j + n_new)


def _conv_decode(u, state, layer, w, cb, lg, lb, per_step=16):
    n_new, n_seq, _ = u.shape
    ctx = CONV_W - 1
    ublk = pl.BlockSpec((n_new, per_step, C_CH), lambda i: (0, i, 0))
    vec = _resident((1, C_CH))
    return pl.pallas_call(
        functools.partial(_conv_decode_body, n_new=n_new),
        grid=(n_seq // per_step,),
        in_specs=[ublk, pl.BlockSpec((None, ctx, per_step, C_CH), lambda i: (layer, 0, i, 0)),
                  _resident((CONV_W, C_CH)), vec, vec, vec],
        out_specs=[ublk, pl.BlockSpec((ctx, per_step, C_CH), lambda i: (0, i, 0))],
        out_shape=[jax.ShapeDtypeStruct((n_new, n_seq, C_CH), F32), jax.ShapeDtypeStruct((ctx, n_seq, C_CH), F32)],
        compiler_params=_params("parallel"),
        name="conv_decode",
    )(u, state, w, cb, lg, lb)


def _mix_ffn_body(x_ref, oa_ref, ob_ref, oc_ref, gt_ref, wpa_ref, wpb_ref, wpc_ref, wo_ref, gm_ref, g1_ref, w1_ref,
                  w2_ref, g2_ref, y_ref, *, chunk):
    gate = lambda j: gt_ref[:, j * D_MODEL:(j + 1) * D_MODEL].astype(F32)
    m = (gate(0) * _dot(oa_ref[...], wpa_ref[...]) + gate(1) * _dot(ob_ref[...], wpb_ref[...])
         + gate(2) * _dot(oc_ref[...], wpc_ref[...]))
    x = x_ref[...] + _rms(_dot(m.astype(BF16), wo_ref[...]), gm_ref[...])
    h = _rms(x, g1_ref[...]).astype(BF16)
    acc = None
    for c0 in range(0, D_FF, chunk):
        a = jnp.maximum(_dot(h, w1_ref[:, c0:c0 + chunk]), 0.0)
        part = _dot((a * a).astype(BF16), w2_ref[c0:c0 + chunk, :])
        acc = part if acc is None else acc + part
    y_ref[...] = x + _rms(acc, g2_ref[...])


def _mix_ffn(x, oa, ob, oc, gt, wpa, wpb, wpc, wo, gm, g1, w1, w2, g2, tm, chunk=1024):
    n = x.shape[0]
    row = lambda width: pl.BlockSpec((tm, width), lambda i: (i, 0))
    vec = _resident((1, D_MODEL))
    return pl.pallas_call(
        functools.partial(_mix_ffn_body, chunk=chunk),
        grid=(n // tm,),
        in_specs=[row(D_MODEL), row(A_WIDTH), row(SB_WIDTH), row(C_CH), row(N_BRANCH * D_MODEL),
                  _resident(wpa.shape), _resident(wpb.shape), _resident(wpc.shape), _resident(wo.shape), vec,
                  vec, _resident(w1.shape), _resident(w2.shape), vec],
        out_specs=row(D_MODEL),
        out_shape=jax.ShapeDtypeStruct((n, D_MODEL), F32),
        compiler_params=_params("parallel"),
        name="mix_ffn",
    )(x, oa, ob, oc, gt, wpa, wpb, wpc, wo, gm, g1, w1, w2, g2)


def _trunk(x3, wts, lb, cache):
    batch, seq, _ = x3.shape
    n = batch * seq
    x = x3.reshape(n, D_MODEL)
    tm = SB_TILE
    depth = wts["w_in"].shape[0]
    ks, vs, ss, cs = [], [], [], []
    kv_stack = s_stack = None
    for l in range(depth):
        vec = lambda name: wts[name][l].reshape(1, -1)
        conv_args = (wts["conv_w"][l], vec("conv_b"), vec("ln_g"), vec("ln_b"))
        proj_args = (x, vec("g_pre_mix"), wts["w_in"][l], lb[l].reshape(1, -1), l, tm)
        nbias = -LOG2E * wts["sb_bias"][l]
        if cache is None:
            qa, lf, ia, ga, qb, u, gt, kt, vt, kth, vth = _in_proj(*proj_args, batch=batch, depth=depth,
                                                                   kv_stack=kv_stack)
            kv_stack = (kt, vt)
            oa, s_new = _hgrn_prompt(qa, lf, ia, ga, vec("a_norm"), batch)
            ss.append(s_new)
            ob = _sb_prompt(qb.reshape(batch, seq, SB_WIDTH), kth, vth, nbias).reshape(n, SB_WIDTH)
            oc = _conv_prompt(u, *conv_args, batch)
            cs.append(u.reshape(batch, seq, C_CH)[:, seq - (CONV_W - 1):])
        else:
            cache_kt, cache_vt, page_table, state_hgrn, state_conv_t = cache
            qa, lf, ia, ga, qb, u, gt, kb, vb = _in_proj(*proj_args)
            oa, s_stack = _hgrn_decode(qa, lf, ia, ga, state_hgrn, l, vec("a_norm"), seq, s_stack)
            bias_rows = jnp.broadcast_to(jnp.tile(nbias, seq)[:, None], (seq * SB_HEADS, PAGE_SIZE))
            tok = lambda a: a.reshape(batch, seq, -1)
            ob = _sb_decode(tok(qb).astype(F32), tok(kb), tok(vb), cache_kt, cache_vt, l, page_table, bias_rows)
            ob = ob.reshape(n, SB_WIDTH).astype(BF16)
            oc, c_new = _conv_decode(tok(u).transpose(1, 0, 2), state_conv_t, l, *conv_args)
            oc = oc.transpose(1, 0, 2).reshape(n, C_CH).astype(BF16)
            cs.append(c_new.transpose(1, 0, 2))
            ks.append(kb.reshape(batch, seq, SB_HEADS, SB_HEAD_DIM))
            vs.append(vb.reshape(batch, seq, SB_HEADS, SB_HEAD_DIM))
        x = _mix_ffn(x, oa, ob, oc, gt, wts["w_pa"][l], wts["w_pb"][l], wts["w_pc"][l], wts["w_o"][l],
                     vec("g_post_mix"), vec("g_pre_ffn"), wts["w_ff1"][l], wts["w_ff2"][l], vec("g_post_ffn"),
                     MIX_FFN_ROWS)
    if cache is None:
        k_all, v_all = (a.reshape(depth, batch, SB_HEADS, SB_HEAD_DIM, seq).transpose(0, 1, 4, 2, 3) for a in kv_stack)
        s_all = jnp.stack(ss)
    else:
        k_all, v_all, s_all = jnp.stack(ks), jnp.stack(vs), s_stack
    return x.reshape(batch, seq, D_MODEL), k_all, v_all, s_all, jnp.stack(cs)


def kernel(x_prompt, x_sample, cache_k, cache_v, page_table, state_hgrn, state_conv, g_pre_mix, g_post_mix,
           g_pre_ffn, g_post_ffn, w_in, hgrn_lb, a_norm, w_pa, w_pb, sb_bias, conv_w, conv_b, ln_g, ln_b, w_pc,
           w_o, w_ff1, w_ff2):
    wts = dict(g_pre_mix=g_pre_mix, g_post_mix=g_post_mix, g_pre_ffn=g_pre_ffn, g_post_ffn=g_post_ffn,
               a_norm=a_norm, sb_bias=sb_bias, conv_w=conv_w, conv_b=conv_b, ln_g=ln_g, ln_b=ln_b,
               w_in=w_in.astype(BF16), w_pa=w_pa.astype(BF16), w_pb=w_pb.astype(BF16), w_pc=w_pc.astype(BF16),
               w_o=w_o.astype(BF16), w_ff1=w_ff1.astype(BF16), w_ff2=w_ff2.astype(BF16))
    lb = jnp.cumsum(jax.nn.softmax(hgrn_lb.astype(F32), axis=0), axis=0)
    lb = lb - lb[0:1]
    y_p, k_p, v_p, s_p, c_p = _trunk(x_prompt, wts, lb, None)
    cache = (cache_k.transpose(0, 1, 3, 4, 2), cache_v.transpose(0, 1, 3, 4, 2), page_table, state_hgrn,
             state_conv.transpose(0, 2, 1, 3))
    y_s, k_s, v_s, s_s, c_s = _trunk(x_sample, wts, lb, cache)
    return (y_p, y_s, k_p, v_p, s_p, c_p, k_s, v_s, s_s, c_s)
```
